```python
import jax, jax.numpy as jnp
from jax import lax
import numpy as np

D_MODEL = 2048
BATCH = 8
SEQ = 2048
DEPTH = 1

NSA_HEADS = 8
NSA_KV_GROUPS = 2
NSA_HEAD_DIM = 128
CMP_BLOCK = 32
CMP_STRIDE = 16
CMP_HIDDEN = 256
SEL_BLOCK = 64
SEL_TOPK = 8
N_LOCAL_BLOCKS = 2
WINDOW = 512
Q_BLOCK = 64
GLA_HEADS = 4
GLA_KEY_DIM = 128
GLA_VAL_DIM = 256
GATE_RANK = 16
GATE_TAU = 16.0
GLA_CHUNK = 64
GLA_SUB = 16
N_EXPERT_GROUPS = 4
EXPERTS_PER_GROUP = 8
TOPK_IN_GROUP = 2
D_EXPERT = 512

EPS = 1e-6
MASK_VALUE = -1e30

NSA_Q_WIDTH = NSA_HEADS * NSA_HEAD_DIM
NSA_KV_WIDTH = NSA_KV_GROUPS * NSA_HEAD_DIM
GLA_QK_WIDTH = GLA_HEADS * GLA_KEY_DIM
GLA_V_WIDTH = GLA_HEADS * GLA_VAL_DIM
MIX_WIDTH = NSA_Q_WIDTH + GLA_V_WIDTH
IN_PROJ_SIZES = (NSA_Q_WIDTH,) + (NSA_KV_WIDTH,) * 6 + (3 * NSA_HEADS, GLA_QK_WIDTH, GLA_QK_WIDTH, GLA_V_WIDTH, GATE_RANK, GLA_V_WIDTH)
IN_PROJ_WIDTH = sum(IN_PROJ_SIZES)

kernel_name = "hymba_nsa_gla_hmoe_alibi"


def rms_norm(x, gain):
    xf = x.astype(jnp.float32)
    y = xf * lax.rsqrt(jnp.mean(jnp.square(xf), axis=-1, keepdims=True) + EPS)
    return (y * gain.astype(jnp.float32)).astype(x.dtype)


def alibi_slopes(n):
    return jnp.asarray(2.0 ** (-8.0 * np.arange(1, n + 1) / n), dtype=jnp.float32)


def compress_blocks(kv, pos_emb, w1, w2):
    s = kv.shape[2]
    n_cmp = (s - CMP_BLOCK) // CMP_STRIDE + 1
    idx = np.arange(n_cmp)[:, None] * CMP_STRIDE + np.arange(CMP_BLOCK)[None, :]
    blocks = kv[:, :, idx] + pos_emb
    flat = blocks.reshape(blocks.shape[:3] + (CMP_BLOCK * NSA_HEAD_DIM,))
    return jax.nn.gelu(flat @ w1) @ w2


def nsa_attention(q, k_cmp, v_cmp, k_sel, v_sel, k_win, v_win, gate_logits,
                  cmp_pos_k, w_cmp_k1, w_cmp_k2, cmp_pos_v, w_cmp_v1, w_cmp_v2):
    b, s = q.shape[:2]
    g, hpg, dk = NSA_KV_GROUPS, NSA_HEADS // NSA_KV_GROUPS, NSA_HEAD_DIM
    q = q.reshape(b, s, g, hpg, dk).transpose(0, 2, 3, 1, 4)
    to_groups = lambda a: a.reshape(b, s, g, dk).transpose(0, 2, 1, 3)
    gates = jax.nn.sigmoid(gate_logits.astype(jnp.float32)).reshape(b, s, g, hpg, 3).transpose(0, 2, 3, 1, 4)

    kc = compress_blocks(to_groups(k_cmp), cmp_pos_k, w_cmp_k1, w_cmp_k2)
    vc = compress_blocks(to_groups(v_cmp), cmp_pos_v, w_cmp_v1, w_cmp_v2)
    n_cmp = kc.shape[2]
    cmp_end_np = np.arange(n_cmp) * CMP_STRIDE + CMP_BLOCK - 1
    cmp_end = jnp.asarray(cmp_end_np, dtype=jnp.int32)
    cmp_end_f = cmp_end.astype(jnp.float32)

    n_sel = s // SEL_BLOCK
    ks_blocks = to_groups(k_sel).reshape(b, g, n_sel, SEL_BLOCK, dk)
    vs_blocks = to_groups(v_sel).reshape(b, g, n_sel, SEL_BLOCK, dk)
    c_start = np.arange(n_cmp)[:, None] * CMP_STRIDE
    s_start = np.arange(n_sel)[None, :] * SEL_BLOCK
    overlap = jnp.asarray(((c_start < s_start + SEL_BLOCK) & (c_start + CMP_BLOCK > s_start)).astype(np.float32))
    topk = min(SEL_TOPK, n_sel)
    sel_start = jnp.arange(n_sel) * SEL_BLOCK
    blk = jnp.arange(n_sel)
    b_idx = jnp.arange(b)[:, None, None, None]
    g_idx = jnp.arange(g)[None, :, None, None]

    pad = ((0, 0), (0, 0), (WINDOW, 0), (0, 0))
    kwp = jnp.pad(to_groups(k_win), pad)
    vwp = jnp.pad(to_groups(v_win), pad)

    slopes = alibi_slopes(NSA_HEADS).reshape(g, hpg)[None, :, :, None, None]
    scale = dk ** -0.5
    nqb = s // Q_BLOCK
    q_blocks = q.reshape(b, g, hpg, nqb, Q_BLOCK, dk).transpose(3, 0, 1, 2, 4, 5)
    g_blocks = gates.reshape(b, g, hpg, nqb, Q_BLOCK, 3).transpose(3, 0, 1, 2, 4, 5)

    def block_fn(args):
        qb, qc, gc = args
        t = qb * Q_BLOCK + jnp.arange(Q_BLOCK)
        tf = t.astype(jnp.float32)
        s_c = jnp.einsum('bghqd,bgnd->bghqn', qc, kc).astype(jnp.float32) * scale
        s_c = s_c - slopes * (tf[:, None] - cmp_end_f[None, :])
        valid_c = cmp_end[None, :] <= t[:, None]
        s_c = jnp.where(valid_c, s_c, MASK_VALUE)
        p_c = jax.nn.softmax(s_c, axis=-1) * jnp.any(valid_c, axis=-1)[:, None].astype(jnp.float32)
        o_c = jnp.einsum('bghqn,bgnd->bghqd', p_c.astype(vc.dtype), vc).astype(jnp.float32)
        imp = jnp.einsum('bghqn,nj->bgqj', p_c, overlap)
        cur = t // SEL_BLOCK
        causal_blk = sel_start[None, :] <= t[:, None]
        forced = (blk[None, :] == 0) | ((blk[None, :] <= cur[:, None]) & (blk[None, :] > cur[:, None] - N_LOCAL_BLOCKS))
        imp = jnp.where(forced, -MASK_VALUE, jnp.where(causal_blk, imp, MASK_VALUE))
        _, sel = lax.top_k(imp, topk)
        k_g = ks_blocks[b_idx, g_idx, sel]
        v_g = vs_blocks[b_idx, g_idx, sel]
        pos = sel[..., None] * SEL_BLOCK + jnp.arange(SEL_BLOCK)
        dist = t[:, None, None] - pos
        s_s = jnp.einsum('bghqd,bgqksd->bghqks', qc, k_g).astype(jnp.float32) * scale
        s_s = s_s - slopes[..., None] * dist[:, :, None].astype(jnp.float32)
        s_s = jnp.where((dist >= 0)[:, :, None], s_s, MASK_VALUE)
        p_s = jax.nn.softmax(s_s.reshape(s_s.shape[:4] + (topk * SEL_BLOCK,)), axis=-1).reshape(s_s.shape)
        o_s = jnp.einsum('bghqks,bgqksd->bghqd', p_s.astype(v_g.dtype), v_g).astype(jnp.float32)
        start = qb * Q_BLOCK
        kw = lax.dynamic_slice_in_dim(kwp, start, WINDOW + Q_BLOCK, axis=2)
        vw = lax.dynamic_slice_in_dim(vwp, start, WINDOW + Q_BLOCK, axis=2)
        pos_w = start - WINDOW + jnp.arange(WINDOW + Q_BLOCK)
        dw = t[:, None] - pos_w[None, :]
        valid_w = (pos_w[None, :] >= 0) & (dw >= 0) & (dw < WINDOW)
        s_w = jnp.einsum('bghqd,bgkd->bghqk', qc, kw).astype(jnp.float32) * scale - slopes * dw.astype(jnp.float32)
        s_w = jnp.where(valid_w, s_w, MASK_VALUE)
        p_w = jax.nn.softmax(s_w, axis=-1)
        o_w = jnp.einsum('bghqk,bgkd->bghqd', p_w.astype(vw.dtype), vw).astype(jnp.float32)
        out = gc[..., 0:1] * o_c + gc[..., 1:2] * o_s + gc[..., 2:3] * o_w
        return out.astype(qc.dtype)

    outs = lax.map(block_fn, (jnp.arange(nqb), q_blocks, g_blocks))
    return outs.transpose(1, 0, 4, 2, 3, 5).reshape(b, s, NSA_Q_WIDTH)


def gla_attention(q, k, v, alpha_lr, out_gate, w_alpha2, b_alpha, g_norm):
    b, s = q.shape[:2]
    h, dk, dv = GLA_HEADS, GLA_KEY_DIM, GLA_VAL_DIM
    out_dtype = q.dtype
    q = q.reshape(b, s, h, dk).astype(jnp.float32) * dk ** -0.5
    k = k.reshape(b, s, h, dk).astype(jnp.float32)
    v = v.reshape(b, s, h, dv).astype(jnp.float32)
    glog = jax.nn.log_sigmoid((alpha_lr @ w_alpha2 + b_alpha).astype(jnp.float32)) / GATE_TAU
    glog = glog.reshape(b, s, h, dk)
    n_chunks = s // GLA_CHUNK
    ns = GLA_CHUNK // GLA_SUB
    to_chunks = lambda a: a.reshape(b, n_chunks, GLA_CHUNK, h, a.shape[-1]).transpose(1, 0, 3, 2, 4)
    tri = np.tril(np.ones((GLA_SUB, GLA_SUB), dtype=bool))
    strict_lower = np.tril(np.ones((ns, ns), dtype=bool), -1)
    eye = jnp.asarray(np.eye(ns, dtype=np.float32))

    def step(state, inp):
        qc, kc, vc, gc = inp
        bcum = jnp.cumsum(gc, axis=2)
        b_last = bcum[:, :, -1]
        o_inter = jnp.einsum('bhcd,bhde->bhce', qc * jnp.exp(bcum), state)
        qs = qc.reshape(b, h, ns, GLA_SUB, dk)
        ks = kc.reshape(b, h, ns, GLA_SUB, dk)
        vs = vc.reshape(b, h, ns, GLA_SUB, dv)
        bs = bcum.reshape(b, h, ns, GLA_SUB, dk)
        diff = bs[:, :, :, :, None, :] - bs[:, :, :, None, :, :]
        diag = jnp.einsum('bhntk,bhnsk,bhntsk->bhnts', qs, ks, jnp.exp(jnp.where(tri[:, :, None], diff, -jnp.inf)))
        r = bs[:, :, :, -1]
        ex = bs[:, :, :, None] - r[:, :, None, :, None]
        qf = qs[:, :, :, None] * jnp.exp(jnp.where(strict_lower[:, :, None, None], ex, -jnp.inf))
        kf = ks * jnp.exp(r[:, :, :, None] - bs)
        a_off = jnp.einsum('bhijtk,bhjsk->bhijts', qf, kf)
        a = a_off + eye[:, :, None, None] * diag[:, :, :, None]
        o_intra = jnp.einsum('bhijts,bhjse->bhite', a, vs).reshape(b, h, GLA_CHUNK, dv)
        new_state = state * jnp.exp(b_last)[..., None] + jnp.einsum('bhcd,bhce->bhde', kc * jnp.exp(b_last[:, :, None] - bcum), vc)
        return new_state, o_inter + o_intra

    init = jnp.zeros((b, h, dk, dv), jnp.float32)
    _, o = lax.scan(step, init, (to_chunks(q), to_chunks(k), to_chunks(v), to_chunks(glog)))
    o = o.transpose(1, 0, 3, 2, 4).reshape(b, s, h, dv)
    o = o * lax.rsqrt(jnp.mean(jnp.square(o), axis=-1, keepdims=True) + EPS) * g_norm.astype(jnp.float32)
    o = o.reshape(b, s, GLA_V_WIDTH) * jax.nn.silu(out_gate.astype(jnp.float32))
    return o.astype(out_dtype)


def hierarchical_moe(h, w_rg, b_rg, w_re, b_re, w_gate, w_up, w_down):
    b, s, d = h.shape
    tok = h.reshape(b * s, d)
    gprob = jax.nn.softmax((tok @ w_rg).astype(jnp.float32) + b_rg.astype(jnp.float32), axis=-1)
    gsel = jnp.argmax(gprob, axis=-1)
    gw = jnp.take_along_axis(gprob, gsel[:, None], axis=-1)
    elogits = jnp.einsum('td,dge->tge', tok, w_re).astype(jnp.float32) + b_re.astype(jnp.float32)
    elog_sel = jnp.take_along_axis(elogits, gsel[:, None, None], axis=1)[:, 0]
    top_v, top_i = lax.top_k(jax.nn.softmax(elog_sel, axis=-1), TOPK_IN_GROUP)
    top_v = top_v / jnp.sum(top_v, axis=-1, keepdims=True)
    within = jnp.sum(jax.nn.one_hot(top_i, EXPERTS_PER_GROUP, dtype=jnp.float32) * top_v[..., None], axis=1)
    combine = jax.nn.one_hot(gsel, N_EXPERT_GROUPS, dtype=jnp.float32)[:, :, None] * within[:, None, :] * gw[:, :, None]
    out = jnp.zeros((b * s, d), jnp.float32)
    for gi in range(N_EXPERT_GROUPS):
        hid = jax.nn.silu(jnp.einsum('td,edf->tef', tok, w_gate[gi])) * jnp.einsum('td,edf->tef', tok, w_up[gi])
        hid = hid * combine[:, gi, :, None].astype(hid.dtype)
        out = out + jnp.einsum('tef,efd->td', hid, w_down[gi]).astype(jnp.float32)
    return out.reshape(b, s, d).astype(h.dtype)


def setup_inputs(seed: int = 0) -> dict:
    key = jax.random.key(seed)
    ks = jax.random.split(key, 24)
    L = DEPTH
    dk = NSA_HEAD_DIM
    nrm = lambda k, shape, scale: jax.random.normal(k, shape, jnp.float32) * scale
    return {
        "x": nrm(ks[0], (BATCH, SEQ, D_MODEL), 1.0),
        "g_mix_norm": 1.0 + nrm(ks[1], (L, D_MODEL), 0.02),
        "w_in": nrm(ks[2], (L, D_MODEL, IN_PROJ_WIDTH), D_MODEL ** -0.5),
        "b_nsa_gate": nrm(ks[3], (L, 3 * NSA_HEADS), 0.1),
        "cmp_pos_k": nrm(ks[4], (L, CMP_BLOCK, dk), 0.1),
        "w_cmp_k1": nrm(ks[5], (L, CMP_BLOCK * dk, CMP_HIDDEN), (CMP_BLOCK * dk) ** -0.5),
        "w_cmp_k2": nrm(ks[6], (L, CMP_HIDDEN, dk), CMP_HIDDEN ** -0.5),
        "cmp_pos_v": nrm(ks[7], (L, CMP_BLOCK, dk), 0.1),
        "w_cmp_v1": nrm(ks[8], (L, CMP_BLOCK * dk, CMP_HIDDEN), (CMP_BLOCK * dk) ** -0.5),
        "w_cmp_v2": nrm(ks[9], (L, CMP_HIDDEN, dk), CMP_HIDDEN ** -0.5),
        "w_alpha2": nrm(ks[10], (L, GATE_RANK, GLA_QK_WIDTH), GATE_RANK ** -0.5),
        "b_alpha": nrm(ks[11], (L, GLA_QK_WIDTH), 0.1),
        "g_gla_norm": 1.0 + nrm(ks[12], (L, GLA_VAL_DIM), 0.02),
        "w_out": nrm(ks[13], (L, MIX_WIDTH, D_MODEL), MIX_WIDTH ** -0.5),
        "g_ffn_norm": 1.0 + nrm(ks[14], (L, D_MODEL), 0.02),
        "w_router_group": nrm(ks[15], (L, D_MODEL, N_EXPERT_GROUPS), D_MODEL ** -0.5),
        "b_router_group": nrm(ks[16], (L, N_EXPERT_GROUPS), 0.01),
        "w_router_expert": nrm(ks[17], (L, D_MODEL, N_EXPERT_GROUPS, EXPERTS_PER_GROUP), D_MODEL ** -0.5),
        "b_router_expert": nrm(ks[18], (L, N_EXPERT_GROUPS, EXPERTS_PER_GROUP), 0.01),
        "w_expert_gate": nrm(ks[19], (L, N_EXPERT_GROUPS, EXPERTS_PER_GROUP, D_MODEL, D_EXPERT), D_MODEL ** -0.5),
        "w_expert_up": nrm(ks[20], (L, N_EXPERT_GROUPS, EXPERTS_PER_GROUP, D_MODEL, D_EXPERT), D_MODEL ** -0.5),
        "w_expert_down": nrm(ks[21], (L, N_EXPERT_GROUPS, EXPERTS_PER_GROUP, D_EXPERT, D_MODEL), D_EXPERT ** -0.5),
        "g_final_norm": 1.0 + nrm(ks[22], (D_MODEL,), 0.02),
    }


def reference(x, g_mix_norm, w_in, b_nsa_gate, cmp_pos_k, w_cmp_k1, w_cmp_k2, cmp_pos_v, w_cmp_v1, w_cmp_v2,
              w_alpha2, b_alpha, g_gla_norm, w_out, g_ffn_norm, w_router_group, b_router_group,
              w_router_expert, b_router_expert, w_expert_gate, w_expert_up, w_expert_down, g_final_norm):
    split_points = np.cumsum(IN_PROJ_SIZES)[:-1].tolist()
    for l in range(DEPTH):
        h = rms_norm(x, g_mix_norm[l])
        parts = jnp.split(h @ w_in[l], split_points, axis=-1)
        nsa_q, k_cmp, v_cmp, k_sel, v_sel, k_win, v_win, nsa_gate, gla_q, gla_k, gla_v, gla_alpha_lr, gla_out_gate = parts
        nsa_out = nsa_attention(nsa_q, k_cmp, v_cmp, k_sel, v_sel, k_win, v_win, nsa_gate + b_nsa_gate[l],
                                cmp_pos_k[l], w_cmp_k1[l], w_cmp_k2[l], cmp_pos_v[l], w_cmp_v1[l], w_cmp_v2[l])
        gla_out = gla_attention(gla_q, gla_k, gla_v, gla_alpha_lr, gla_out_gate, w_alpha2[l], b_alpha[l], g_gla_norm[l])
        x = x + jnp.concatenate([nsa_out, gla_out], axis=-1) @ w_out[l]
        x = x + hierarchical_moe(rms_norm(x, g_ffn_norm[l]), w_router_group[l], b_router_group[l],
                                 w_router_expert[l], b_router_expert[l], w_expert_gate[l], w_expert_up[l], w_expert_down[l])
    return rms_norm(x, g_final_norm)
```

```python
import functools

import numpy as np
import jax
import jax.numpy as jnp
from jax import lax
from jax.experimental import pallas as pl
from jax.experimental.pallas import tpu as pltpu

NSA_HEADS = 8
NSA_KV_GROUPS = 2
HEADS_PER_GROUP = NSA_HEADS // NSA_KV_GROUPS
HEAD_DIM = 128
CMP_BLOCK = 32
CMP_STRIDE = 16
SEL_BLOCK = 64
SEL_TOPK = 8
N_LOCAL_BLOCKS = 2
WINDOW = 512
GLA_HEADS = 4
GLA_KEY_DIM = 128
GLA_VAL_DIM = 256
GATE_RANK = 16
GATE_TAU = 16.0
GLA_CHUNK = 64
GLA_SUB = 16
N_EXPERT_GROUPS = 4
EXPERTS_PER_GROUP = 8
N_EXPERTS = N_EXPERT_GROUPS * EXPERTS_PER_GROUP
EPS = 1e-6
MASK_VALUE = -1e30

NSA_Q_WIDTH = NSA_HEADS * HEAD_DIM
NSA_KV_WIDTH = NSA_KV_GROUPS * HEAD_DIM
GLA_QK_WIDTH = GLA_HEADS * GLA_KEY_DIM
GLA_V_WIDTH = GLA_HEADS * GLA_VAL_DIM

LANES = 128
VMEM_LIMIT_BYTES = 56 * 1024 * 1024

COL_NSA_Q = 0
COL_KV = COL_NSA_Q + NSA_Q_WIDTH
COL_GLA_Q = COL_KV + 6 * NSA_KV_WIDTH
COL_GLA_K = COL_GLA_Q + GLA_QK_WIDTH
COL_GLA_V = COL_GLA_K + GLA_QK_WIDTH
COL_GLA_GATE = COL_GLA_V + GLA_V_WIDTH
BIG_WIDTH = COL_GLA_GATE + GLA_V_WIDTH
SMALL_GATE = 0
SMALL_ALPHA = 3 * NSA_HEADS
SMALL_WIDTH = LANES


def _params(semantics):
    return pltpu.CompilerParams(dimension_semantics=semantics, vmem_limit_bytes=VMEM_LIMIT_BYTES)


def _split3(a):
    hi = a.astype(jnp.bfloat16)
    r1 = a - hi.astype(jnp.float32)
    mid = r1.astype(jnp.bfloat16)
    lo = (r1 - mid.astype(jnp.float32)).astype(jnp.bfloat16)
    return hi, mid, lo


def _in_proj_kernel(x_ref, g_ref, wb_ref, ws_ref, scale_ref, sbias_ref, big_ref, small_ref, h_ref):
    n = pl.program_id(1)

    @pl.when(n == 0)
    def _():
        xf = x_ref[...]
        y = xf * lax.rsqrt(jnp.mean(xf * xf, axis=-1, keepdims=True) + EPS)
        h = (y * g_ref[...]).astype(jnp.bfloat16)
        h_ref[...] = h
        small_ref[...] = jnp.dot(h, ws_ref[...], preferred_element_type=jnp.float32) + sbias_ref[...]

    acc = jnp.dot(h_ref[...], wb_ref[...], preferred_element_type=jnp.float32)
    big_ref[...] = (acc * scale_ref[...]).astype(big_ref.dtype)


def _in_proj(x2, gain, w_big, w_small, col_scale, small_bias, tm=1024, tn=512):
    t, d = x2.shape
    tm = min(tm, t)
    grid = (t // tm, BIG_WIDTH // tn)
    return pl.pallas_call(
        _in_proj_kernel,
        grid=grid,
        in_specs=[
            pl.BlockSpec((tm, d), lambda m, n: (m, 0)),
            pl.BlockSpec((1, d), lambda m, n: (0, 0)),
            pl.BlockSpec((d, tn), lambda m, n: (0, n)),
            pl.BlockSpec((d, SMALL_WIDTH), lambda m, n: (0, 0)),
            pl.BlockSpec((1, tn), lambda m, n: (0, n)),
            pl.BlockSpec((1, SMALL_WIDTH), lambda m, n: (0, 0)),
        ],
        out_specs=[
            pl.BlockSpec((tm, tn), lambda m, n: (m, n)),
            pl.BlockSpec((tm, SMALL_WIDTH), lambda m, n: (m, 0)),
        ],
        out_shape=[
            jax.ShapeDtypeStruct((t, BIG_WIDTH), jnp.bfloat16),
            jax.ShapeDtypeStruct((t, SMALL_WIDTH), jnp.float32),
        ],
        scratch_shapes=[pltpu.VMEM((tm, d), jnp.bfloat16)],
        compiler_params=_params(("parallel", "arbitrary")),
    )(x2, gain, w_big, w_small, col_scale, small_bias)


def _prep_in_proj_weights(w_in, b_nsa_gate):
    sizes = (NSA_Q_WIDTH,) + (NSA_KV_WIDTH,) * 6 + (3 * NSA_HEADS, GLA_QK_WIDTH, GLA_QK_WIDTH,
                                                    GLA_V_WIDTH, GATE_RANK, GLA_V_WIDTH)
    offs = np.concatenate([[0], np.cumsum(sizes)])
    part = lambda i: w_in[:, offs[i]:offs[i + 1]]
    w_big = jnp.concatenate([part(i) for i in (0, 1, 2, 3, 4, 5, 6, 8, 9, 10, 12)], axis=1)
    pad = jnp.zeros((w_in.shape[0], SMALL_WIDTH - 3 * NSA_HEADS - GATE_RANK), w_in.dtype)
    w_small = jnp.concatenate([part(7), part(11), pad], axis=1)
    scale = np.ones((1, BIG_WIDTH), np.float32)
    scale[0, COL_NSA_Q:COL_NSA_Q + NSA_Q_WIDTH] = HEAD_DIM ** -0.5
    scale[0, COL_GLA_Q:COL_GLA_Q + GLA_QK_WIDTH] = GLA_KEY_DIM ** -0.5
    small_bias = jnp.concatenate([b_nsa_gate.astype(jnp.float32),
                                  jnp.zeros((SMALL_WIDTH - 3 * NSA_HEADS,), jnp.float32)])[None]
    return w_big.astype(jnp.bfloat16), w_small.astype(jnp.bfloat16), jnp.asarray(scale), small_bias


def _gelu_tanh(a):
    return 0.5 * a * (1.0 + jnp.tanh(np.sqrt(2.0 / np.pi) * (a + 0.044715 * (a * a * a))))


def _compress_kernel(z_ref, pos_ref, w1_ref, w2_ref, o_ref):
    z = z_ref[0].astype(jnp.float32)
    pos = pos_ref[0]
    z_lo = (z + pos[0:1]).astype(jnp.bfloat16)
    z_hi = (z + pos[1:2]).astype(jnp.bfloat16)
    u = jnp.dot(z_lo, w1_ref[0, 0], preferred_element_type=jnp.float32)
    v = jnp.dot(z_hi, w1_ref[0, 1], preferred_element_type=jnp.float32)
    a = u + pltpu.roll(v, v.shape[0] - 1, axis=0)
    hid = _gelu_tanh(a).astype(jnp.bfloat16)
    o_ref[0] = jnp.dot(hid, w2_ref[0], preferred_element_type=jnp.float32).astype(o_ref.dtype)


def _compress(z, pos, w1, w2, tr):
    _, rows, width = z.shape
    hid = w1.shape[-1]
    return pl.pallas_call(
        _compress_kernel,
        grid=(2, rows // tr),
        in_specs=[
            pl.BlockSpec((1, tr, width), lambda c, r: (c, r, 0)),
            pl.BlockSpec((1, 2, width), lambda c, r: (c, 0, 0)),
            pl.BlockSpec((1, 2, width, hid), lambda c, r: (c, 0, 0, 0)),
            pl.BlockSpec((1, hid, HEAD_DIM), lambda c, r: (c, 0, 0)),
        ],
        out_specs=pl.BlockSpec((1, tr, HEAD_DIM), lambda c, r: (c, r, 0)),
        out_shape=jax.ShapeDtypeStruct((2, rows, HEAD_DIM), jnp.bfloat16),
        compiler_params=_params(("parallel", "parallel")),
    )(z, pos, w1, w2)


def _compress_stage(big, b, s, cmp_pos_k, w_cmp_k1, w_cmp_k2, cmp_pos_v, w_cmp_v1, w_cmp_v2):
    g, dk = NSA_KV_GROUPS, HEAD_DIM
    n_seg = s // CMP_STRIDE
    kv = big[:, COL_KV:COL_KV + 2 * NSA_KV_WIDTH].reshape(b, n_seg, CMP_STRIDE, 2, g, dk)
    z = kv.transpose(3, 0, 4, 1, 2, 5).reshape(2, b * g * n_seg, CMP_STRIDE * dk)
    pos = jnp.stack([cmp_pos_k, cmp_pos_v]).reshape(2, 2, CMP_STRIDE * dk)
    w1 = jnp.stack([w_cmp_k1, w_cmp_v1]).reshape(2, 2, CMP_STRIDE * dk, -1).astype(jnp.bfloat16)
    w2 = jnp.stack([w_cmp_k2, w_cmp_v2]).astype(jnp.bfloat16)
    pairs = b * g
    per_tile = max(p for p in (1, 2, 4, 8) if pairs % p == 0 and p * n_seg <= 1024)
    out = _compress(z, pos, w1, w2, per_tile * n_seg)
    return out.reshape(2, b * g, n_seg, dk)


_NT = (((1,), (1,)), ((), ()))


def _nsa_kernel(slopes_ref, q_ref, kc_ref, vct_ref, ksel_ref, vselt_ref, kwin_ref, vwint_ref,
                gate_ref, ovl_ref, o_ref, selb_ref, *, tq, tk):
    g = pl.program_id(1)
    qi = pl.program_id(2)
    q0 = qi * tq
    n_seg = kc_ref.shape[1]
    n_sel = ovl_ref.shape[0]
    mxu_dtype = ksel_ref.dtype
    f32 = jnp.float32
    hd = HEAD_DIM

    t_row = q0 + lax.broadcasted_iota(jnp.int32, (1, tq), 1)

    cmp_end = lax.broadcasted_iota(jnp.int32, (n_seg, 1), 0) * CMP_STRIDE + (CMP_BLOCK - 1)
    valid_c = cmp_end <= t_row
    dist_c = (t_row - cmp_end).astype(f32)
    any_c = (t_row >= CMP_BLOCK - 1).astype(f32)
    kc = kc_ref[0]
    vct = vct_ref[0]
    psum = jnp.zeros((n_seg, tq), f32)
    o_cmp = []
    for hh in range(HEADS_PER_GROUP):
        slope = slopes_ref[g, hh]
        qh = q_ref[0, :, hh * hd:(hh + 1) * hd]
        sc = lax.dot_general(kc, qh, _NT, preferred_element_type=f32)
        sc = jnp.where(valid_c, sc - slope * dist_c, MASK_VALUE)
        m = jnp.max(sc, axis=0, keepdims=True)
        p = jnp.exp(sc - m)
        p = p * (any_c / jnp.sum(p, axis=0, keepdims=True))
        psum = psum + p
        o_cmp.append(jnp.dot(vct, p.astype(mxu_dtype), preferred_element_type=f32))

    ovl = ovl_ref[...]
    imp = jnp.zeros((n_sel, tq), f32)
    for piece in _split3(psum):
        imp = imp + jnp.dot(ovl, piece, preferred_element_type=f32)
    blk = lax.broadcasted_iota(jnp.int32, (n_sel, 1), 0)
    cur = t_row // SEL_BLOCK
    causal_blk = blk * SEL_BLOCK <= t_row
    forced = (blk == 0) | ((blk <= cur) & (blk > cur - N_LOCAL_BLOCKS))
    imp = jnp.where(forced, -MASK_VALUE, jnp.where(causal_blk, imp, MASK_VALUE))
    cnt = jnp.zeros((n_sel, tq), f32)
    for i in range(n_sel):
        ri = imp[i:i + 1, :]
        beats = jnp.where(blk > i, (ri >= imp).astype(f32), (ri > imp).astype(f32))
        cnt = cnt + beats
    topk = min(SEL_TOPK, n_sel)
    selb_ref[...] = jnp.where(cnt < topk, 0.0, MASK_VALUE)

    d0 = (lax.broadcasted_iota(jnp.int32, (tk, tq), 1)
          - lax.broadcasted_iota(jnp.int32, (tk, tq), 0))
    blocks_per_chunk = tk // SEL_BLOCK

    def flash(k_ref, vt_ref, qh, slope, c_lo, c_hi, window):
        def body(c, carry):
            m, l, acc = carry
            k0 = pl.multiple_of(c * tk, tk)
            kblk = k_ref[0, pl.ds(k0, tk), :]
            s = lax.dot_general(kblk, qh, _NT, preferred_element_type=f32)
            dist = d0 + (q0 - k0)
            s = s - slope * dist.astype(f32)
            if window:
                s = jnp.where(dist >= 0, jnp.where(dist < WINDOW, s, MASK_VALUE), MASK_VALUE)
            else:
                s = jnp.where(dist >= 0, s, MASK_VALUE)
                rows = [jnp.broadcast_to(selb_ref[pl.ds(c * blocks_per_chunk + j, 1), :],
                                         (SEL_BLOCK, tq)) for j in range(blocks_per_chunk)]
                s = s + jnp.concatenate(rows, axis=0)
            m_new = jnp.maximum(m, jnp.max(s, axis=0, keepdims=True))
            alpha = jnp.exp(m - m_new)
            p = jnp.exp(s - m_new)
            l = alpha * l + jnp.sum(p, axis=0, keepdims=True)
            vt = vt_ref[0, :, pl.ds(k0, tk)]
            acc = alpha * acc + jnp.dot(vt, p.astype(mxu_dtype), preferred_element_type=f32)
            return m_new, l, acc

        init = (jnp.full((1, tq), MASK_VALUE, f32), jnp.zeros((1, tq), f32), jnp.zeros((hd, tq), f32))
        _, l, acc = lax.fori_loop(c_lo, c_hi, body, init)
        return acc / l

    c_hi = (q0 + tq) // tk
    c_win = jnp.maximum(0, (q0 - WINDOW + 1) // tk)
    for hh in range(HEADS_PER_GROUP):
        slope = slopes_ref[g, hh]
        qh = q_ref[0, :, hh * hd:(hh + 1) * hd]
        o_sel = flash(ksel_ref, vselt_ref, qh, slope, 0, c_hi, False)
        o_win = flash(kwin_ref, vwint_ref, qh, slope, c_win, c_hi, True)
        gates = jax.nn.sigmoid(gate_ref[0, 0, 3 * hh:3 * hh + 3, :])
        out_t = gates[0:1] * o_cmp[hh] + gates[1:2] * o_sel + gates[2:3] * o_win
        o_ref[0, :, hh * hd:(hh + 1) * hd] = out_t.T.astype(o_ref.dtype)


def _nsa_stage(big, small, cmp, b, s, tq=256):
    g, hd = NSA_KV_GROUPS, HEAD_DIM
    tq = min(tq, s)
    tk = tq
    n_seg = s // CMP_STRIDE
    n_sel = s // SEL_BLOCK
    big3 = big.reshape(b, s, BIG_WIDTH)
    kv_col = lambda which: (COL_KV + which * NSA_KV_WIDTH) // hd
    vt = lambda which: big3[:, :, COL_KV + which * NSA_KV_WIDTH:COL_KV + (which + 1) * NSA_KV_WIDTH] \
        .reshape(b, s, g, hd).transpose(0, 2, 3, 1).reshape(b * g, hd, s)
    vsel_t, vwin_t = vt(3), vt(5)
    kc = cmp[0]
    vc_t = cmp[1].transpose(0, 2, 1)
    gate_t = small[:, SMALL_GATE:SMALL_GATE + 3 * NSA_HEADS].reshape(b, s, g, 3 * HEADS_PER_GROUP) \
        .transpose(0, 2, 3, 1)
    c_start = np.arange(n_seg)[None, :] * CMP_STRIDE
    s_start = np.arange(n_sel)[:, None] * SEL_BLOCK
    n_cmp = (s - CMP_BLOCK) // CMP_STRIDE + 1
    ovl = ((c_start < s_start + SEL_BLOCK) & (c_start + CMP_BLOCK > s_start)
           & (np.arange(n_seg)[None, :] < n_cmp)).astype(np.float32)
    ovl = jnp.asarray(ovl, jnp.bfloat16)
    slopes = jnp.asarray((2.0 ** (-8.0 * np.arange(1, NSA_HEADS + 1) / NSA_HEADS))
                         .reshape(g, HEADS_PER_GROUP), jnp.float32)
    gw = 3 * HEADS_PER_GROUP
    kernel = functools.partial(_nsa_kernel, tq=tq, tk=tk)
    return pl.pallas_call(
        kernel,
        grid=(b, g, s // tq),
        in_specs=[
            pl.BlockSpec(memory_space=pltpu.SMEM),
            pl.BlockSpec((1, tq, HEADS_PER_GROUP * hd), lambda bi, gi, qi: (bi, qi, gi)),
            pl.BlockSpec((1, n_seg, hd), lambda bi, gi, qi: (bi * NSA_KV_GROUPS + gi, 0, 0)),
            pl.BlockSpec((1, hd, n_seg), lambda bi, gi, qi: (bi * NSA_KV_GROUPS + gi, 0, 0)),
            pl.BlockSpec((1, s, hd), lambda bi, gi, qi: (bi, 0, kv_col(2) + gi)),
            pl.BlockSpec((1, hd, s), lambda bi, gi, qi: (bi * NSA_KV_GROUPS + gi, 0, 0)),
            pl.BlockSpec((1, s, hd), lambda bi, gi, qi: (bi, 0, kv_col(4) + gi)),
            pl.BlockSpec((1, hd, s), lambda bi, gi, qi: (bi * NSA_KV_GROUPS + gi, 0, 0)),
            pl.BlockSpec((1, 1, gw, tq), lambda bi, gi, qi: (bi, gi, 0, qi)),
            pl.BlockSpec((n_sel, n_seg), lambda bi, gi, qi: (0, 0)),
        ],
        out_specs=pl.BlockSpec((1, tq, HEADS_PER_GROUP * hd), lambda bi, gi, qi: (bi, qi, gi)),
        out_shape=jax.ShapeDtypeStruct((b, s, NSA_Q_WIDTH), big.dtype),
        scratch_shapes=[pltpu.VMEM((n_sel, tq), jnp.float32)],
        compiler_params=_params(("parallel", "parallel", "arbitrary")),
    )(slopes, big3, kc, vc_t, big3, vsel_t, big3, vwin_t, gate_t, ovl)


_TN = (((0,), (0,)), ((), ()))
_HI = lax.Precision.HIGHEST


def _gla_kernel(q_ref, k_ref, v_ref, og_ref, sm_ref, wa_ref, ba_ref, gn_ref, selw_ref, o_ref,
                state_ref, *, n_chunks):
    f32 = jnp.float32
    mxu_dtype = v_ref.dtype
    c_len, sub = GLA_CHUNK, GLA_SUB
    ns = c_len // sub
    dk = GLA_KEY_DIM

    @pl.when(pl.program_id(2) == 0)
    def _():
        state_ref[...] = jnp.zeros_like(state_ref)

    row = lax.broadcasted_iota(jnp.int32, (c_len, 1), 0)
    col = lax.broadcasted_iota(jnp.int32, (1, c_len), 1)
    tril = (col <= row).astype(f32)
    sub_row = row // sub
    sub_col = col // sub
    t_loc = lax.broadcasted_iota(jnp.int32, (1, sub, 1), 1)
    neg_inf = -jnp.inf

    for c in range(n_chunks):
        rows = slice(c * c_len, (c + 1) * c_len)
        q = q_ref[0, rows, :].astype(f32)
        k = k_ref[0, rows, :].astype(f32)
        v = v_ref[0, rows, :]
        logits = jnp.dot(sm_ref[0, rows, :], wa_ref[0], precision=_HI,
                         preferred_element_type=f32) + ba_ref[0]
        glog = (jnp.minimum(logits, 0.0) - jnp.log(1.0 + jnp.exp(-jnp.abs(logits)))) * (1.0 / GATE_TAU)
        bcum = jnp.dot(tril, glog, precision=_HI, preferred_element_type=f32)
        b_last = bcum[c_len - 1:c_len, :]
        st = state_ref[...]

        q_in = (q * jnp.exp(bcum)).astype(mxu_dtype)
        o = lax.dot_general(q_in, st.astype(mxu_dtype), _NT, preferred_element_type=f32)

        r = [bcum[(j + 1) * sub - 1:(j + 1) * sub, :] for j in range(ns)]
        r_rows = jnp.concatenate([jnp.broadcast_to(rj, (sub, dk)) for rj in r], axis=0)
        kf = (k * jnp.exp(r_rows - bcum)).astype(mxu_dtype)
        a = jnp.zeros((c_len, c_len), f32)
        for j in range(ns - 1):
            arg = jnp.where(row >= (j + 1) * sub, bcum - r[j], neg_inf)
            qf = (q * jnp.exp(arg)).astype(mxu_dtype)
            aj = lax.dot_general(qf, kf, _NT, preferred_element_type=f32)
            a = a + jnp.where(sub_col == j, aj, 0.0)

        q3 = q.reshape(ns, sub, dk)
        k3 = k.reshape(ns, sub, dk)
        b3 = bcum.reshape(ns, sub, dk)
        pieces = []
        for s_ in range(sub):
            arg = jnp.where(t_loc >= s_, b3 - b3[:, s_:s_ + 1, :], neg_inf)
            e = q3 * k3[:, s_:s_ + 1, :] * jnp.exp(arg)
            pieces.append(e.reshape(c_len, dk).astype(mxu_dtype))
        d_wide = jnp.dot(jnp.concatenate(pieces, axis=1), selw_ref[...], preferred_element_type=f32)
        a = a + jnp.where(sub_row == sub_col, d_wide, 0.0)
        o = o + jnp.dot(a.astype(mxu_dtype), v, preferred_element_type=f32)

        kd = (k * jnp.exp(b_last - bcum)).astype(mxu_dtype)
        state_ref[...] = st * jnp.exp(b_last) + lax.dot_general(v, kd, _TN, preferred_element_type=f32)

        rms = lax.rsqrt(jnp.mean(o * o, axis=-1, keepdims=True) + EPS)
        gate = og_ref[0, rows, :].astype(f32)
        y = o * rms * gn_ref[...] * (gate * jax.nn.sigmoid(gate))
        o_ref[0, rows, :] = y.astype(o_ref.dtype)


def _gla_stage(big, small, w_alpha2, b_alpha, g_norm, b, s, tc=256):
    h, dk, dv = GLA_HEADS, GLA_KEY_DIM, GLA_VAL_DIM
    tc = min(tc, s)
    big3 = big.reshape(b, s, BIG_WIDTH)
    small3 = small.reshape(b, s, SMALL_WIDTH)
    wa = jnp.zeros((h, SMALL_WIDTH, dk), jnp.float32).at[:, SMALL_ALPHA:SMALL_ALPHA + GATE_RANK, :].set(
        w_alpha2.astype(jnp.float32).reshape(GATE_RANK, h, dk).transpose(1, 0, 2))
    ba = b_alpha.astype(jnp.float32).reshape(h, 1, dk)
    gn = g_norm.astype(jnp.float32).reshape(1, dv)
    sel = (np.arange(GLA_SUB * dk)[:, None] // dk == np.arange(GLA_CHUNK)[None, :] % GLA_SUB)
    selw = jnp.asarray(sel.astype(np.float32), big.dtype)
    kernel = functools.partial(_gla_kernel, n_chunks=tc // GLA_CHUNK)
    return pl.pallas_call(
        kernel,
        grid=(b, h, s // tc),
        in_specs=[
            pl.BlockSpec((1, tc, dk), lambda bi, hi, ci: (bi, ci, COL_GLA_Q // dk + hi)),
            pl.BlockSpec((1, tc, dk), lambda bi, hi, ci: (bi, ci, COL_GLA_K // dk + hi)),
            pl.BlockSpec((1, tc, dv), lambda bi, hi, ci: (bi, ci, COL_GLA_V // dv + hi)),
            pl.BlockSpec((1, tc, dv), lambda bi, hi, ci: (bi, ci, COL_GLA_GATE // dv + hi)),
            pl.BlockSpec((1, tc, SMALL_WIDTH), lambda bi, hi, ci: (bi, ci, 0)),
            pl.BlockSpec((1, SMALL_WIDTH, dk), lambda bi, hi, ci: (hi, 0, 0)),
            pl.BlockSpec((1, 1, dk), lambda bi, hi, ci: (hi, 0, 0)),
            pl.BlockSpec((1, dv), lambda bi, hi, ci: (0, 0)),
            pl.BlockSpec((GLA_SUB * dk, GLA_CHUNK), lambda bi, hi, ci: (0, 0)),
        ],
        out_specs=pl.BlockSpec((1, tc, dv), lambda bi, hi, ci: (bi, ci, hi)),
        out_shape=jax.ShapeDtypeStruct((b, s, GLA_V_WIDTH), big.dtype),
        scratch_shapes=[pltpu.VMEM((dv, dk), jnp.float32)],
        compiler_params=_params(("parallel", "parallel", "arbitrary")),
    )(big3, big3, big3, big3, small3, wa, ba, gn, selw)


ROUTE_E0 = N_EXPERT_GROUPS


def _out_proj_kernel(x_ref, nsa_ref, gla_ref, wt_ref, wb_ref, g_ref, wr_ref, br_ref,
                     x1_ref, h_ref, comb_ref):
    f32 = jnp.float32
    acc = jnp.dot(nsa_ref[...], wt_ref[...], preferred_element_type=f32)
    acc = acc + jnp.dot(gla_ref[...], wb_ref[...], preferred_element_type=f32)
    x1 = x_ref[...] + acc
    x1_ref[...] = x1
    hf = x1 * lax.rsqrt(jnp.mean(x1 * x1, axis=-1, keepdims=True) + EPS) * g_ref[...]
    h_ref[...] = hf.astype(h_ref.dtype)

    h_hi, h_mid, _ = _split3(hf)
    logits = (jnp.dot(h_hi, wr_ref[0], preferred_element_type=f32)
              + jnp.dot(h_hi, wr_ref[1], preferred_element_type=f32)
              + jnp.dot(h_mid, wr_ref[0], preferred_element_type=f32)) + br_ref[...]

    lane = lax.broadcasted_iota(jnp.int32, (1, LANES), 1).astype(f32)
    big_lane = float(LANES)
    neg_inf = -jnp.inf
    first_argmax = lambda vals, vmax: jnp.min(jnp.where(vals == vmax, lane, big_lane), axis=-1, keepdims=True)
    gl = jnp.where(lane < N_EXPERT_GROUPS, logits, neg_inf)
    gmax = jnp.max(gl, axis=-1, keepdims=True)
    gsel = first_argmax(gl, gmax)
    gw = 1.0 / jnp.sum(jnp.exp(gl - gmax), axis=-1, keepdims=True)
    lo = ROUTE_E0 + EXPERTS_PER_GROUP * gsel
    el = jnp.where(lane >= lo, jnp.where(lane < lo + EXPERTS_PER_GROUP, logits, neg_inf), neg_inf)
    m1 = jnp.max(el, axis=-1, keepdims=True)
    i1 = first_argmax(el, m1)
    el2 = jnp.where(lane == i1, neg_inf, el)
    m2 = jnp.max(el2, axis=-1, keepdims=True)
    i2 = first_argmax(el2, m2)
    e2 = jnp.exp(m2 - m1)
    w1 = gw / (1.0 + e2)
    w2 = w1 * e2
    comb_ref[...] = jnp.where(lane == i1, w1, 0.0) + jnp.where(lane == i2, w2, 0.0)


def _out_proj_stage(x2, nsa, gla, w_out, g_ffn, w_rg, b_rg, w_re, b_re, tm=512):
    t, d = x2.shape
    tm = min(tm, t)
    half = nsa.shape[1]
    w_top = w_out[:half].astype(nsa.dtype)
    w_bot = w_out[half:].astype(nsa.dtype)
    wr = jnp.concatenate([w_rg, w_re.reshape(d, N_EXPERTS),
                          jnp.zeros((d, LANES - ROUTE_E0 - N_EXPERTS), jnp.float32)], axis=1)
    wr_hi, wr_mid, _ = _split3(wr)
    wr2 = jnp.stack([wr_hi, wr_mid])
    br = jnp.concatenate([b_rg, b_re.reshape(N_EXPERTS),
                          jnp.zeros((LANES - ROUTE_E0 - N_EXPERTS,), jnp.float32)])[None]
    row = lambda m: (m, 0)
    fixed = lambda m: (0, 0)
    return pl.pallas_call(
        _out_proj_kernel,
        grid=(t // tm,),
        in_specs=[
            pl.BlockSpec((tm, d), row),
            pl.BlockSpec((tm, half), row),
            pl.BlockSpec((tm, half), row),
            pl.BlockSpec((half, d), fixed),
            pl.BlockSpec((half, d), fixed),
            pl.BlockSpec((1, d), fixed),
            pl.BlockSpec((2, d, LANES), lambda m: (0, 0, 0)),
            pl.BlockSpec((1, LANES), fixed),
        ],
        out_specs=[pl.BlockSpec((tm, d), row), pl.BlockSpec((tm, d), row), pl.BlockSpec((tm, LANES), row)],
        out_shape=[
            jax.ShapeDtypeStruct((t, d), jnp.float32),
            jax.ShapeDtypeStruct((t, d), nsa.dtype),
            jax.ShapeDtypeStruct((t, LANES), jnp.float32),
        ],
        compiler_params=_params(("parallel",)),
    )(x2, nsa, gla, w_top, w_bot, g_ffn.astype(jnp.float32)[None], wr2, br)


def _moe_kernel(h_ref, comb_ref, x1_ref, wg_ref, wu_ref, wd_ref, gf_ref, o_ref, acc_ref, *, final_norm):
    f32 = jnp.float32
    e = pl.program_id(1)

    @pl.when(e == 0)
    def _():
        acc_ref[...] = jnp.zeros_like(acc_ref)

    xb = h_ref[...]
    gate = jnp.dot(xb, wg_ref[0], preferred_element_type=f32)
    up = jnp.dot(xb, wu_ref[0], preferred_element_type=f32)
    lane = lax.broadcasted_iota(jnp.int32, (1, LANES), 1)
    cw = jnp.sum(jnp.where(lane == ROUTE_E0 + e, comb_ref[...], 0.0), axis=-1, keepdims=True)
    hid = (gate * jax.nn.sigmoid(gate) * up * cw).astype(xb.dtype)
    acc_ref[...] += jnp.dot(hid, wd_ref[0], preferred_element_type=f32)

    @pl.when(e == pl.num_programs(1) - 1)
    def _():
        y = x1_ref[...] + acc_ref[...]
        if final_norm:
            y = y * lax.rsqrt(jnp.mean(y * y, axis=-1, keepdims=True) + EPS) * gf_ref[...]
        o_ref[...] = y


def _moe_stage(h2, comb, x1, w_gate, w_up, w_down, g_final, final_norm, tm=512):
    t, d = h2.shape
    tm = min(tm, t)
    f = w_gate.shape[-1]
    wg = w_gate.reshape(N_EXPERTS, d, f).astype(h2.dtype)
    wu = w_up.reshape(N_EXPERTS, d, f).astype(h2.dtype)
    wd = w_down.reshape(N_EXPERTS, f, d).astype(h2.dtype)
    row = lambda m, e: (m, 0)
    return pl.pallas_call(
        functools.partial(_moe_kernel, final_norm=final_norm),
        grid=(t // tm, N_EXPERTS),
        in_specs=[
            pl.BlockSpec((tm, d), row),
            pl.BlockSpec((tm, LANES), row),
            pl.BlockSpec((tm, d), row),
            pl.BlockSpec((1, d, f), lambda m, e: (e, 0, 0)),
            pl.BlockSpec((1, d, f), lambda m, e: (e, 0, 0)),
            pl.BlockSpec((1, f, d), lambda m, e: (e, 0, 0)),
            pl.BlockSpec((1, d), lambda m, e: (0, 0)),
        ],
        out_specs=pl.BlockSpec((tm, d), row),
        out_shape=jax.ShapeDtypeStruct((t, d), jnp.float32),
        scratch_shapes=[pltpu.VMEM((tm, d), jnp.float32)],
        compiler_params=_params(("parallel", "arbitrary")),
    )(h2, comb, x1, wg, wu, wd, g_final.astype(jnp.float32)[None])


def kernel(x, g_mix_norm, w_in, b_nsa_gate, cmp_pos_k, w_cmp_k1, w_cmp_k2, cmp_pos_v, w_cmp_v1, w_cmp_v2,
           w_alpha2, b_alpha, g_gla_norm, w_out, g_ffn_norm, w_router_group, b_router_group,
           w_router_expert, b_router_expert, w_expert_gate, w_expert_up, w_expert_down, g_final_norm):
    b, s, d = x.shape
    depth = w_in.shape[0]
    x2 = x.reshape(b * s, d)
    for l in range(depth):
        last = l == depth - 1
        w_big, w_small, col_scale, small_bias = _prep_in_proj_weights(w_in[l], b_nsa_gate[l])
        big, small = _in_proj(x2, g_mix_norm[l].astype(jnp.float32)[None], w_big, w_small, col_scale, small_bias)
        cmp = _compress_stage(big, b, s, cmp_pos_k[l], w_cmp_k1[l], w_cmp_k2[l],
                              cmp_pos_v[l], w_cmp_v1[l], w_cmp_v2[l])
        nsa = _nsa_stage(big, small, cmp, b, s).reshape(b * s, NSA_Q_WIDTH)
        gla = _gla_stage(big, small, w_alpha2[l], b_alpha[l], g_gla_norm[l], b, s).reshape(b * s, GLA_V_WIDTH)
        x1, h2, comb = _out_proj_stage(x2, nsa, gla, w_out[l], g_ffn_norm[l], w_router_group[l],
                                       b_router_group[l], w_router_expert[l], b_router_expert[l])
        x2 = _moe_stage(h2, comb, x1, w_expert_gate[l], w_expert_up[l], w_expert_down[l], g_final_norm, last)
    return x2.reshape(b, s, d)
```

```python
import functools

import numpy as np
import jax
import jax.numpy as jnp
from jax import lax
from jax.experimental import pallas as pl
from jax.experimental.pallas import tpu as pltpu

NSA_HEADS = 8
NSA_KV_GROUPS = 2
HEADS_PER_GROUP = NSA_HEADS // NSA_KV_GROUPS
HEAD_DIM = 128
CMP_BLOCK = 32
CMP_STRIDE = 16
SEL_BLOCK = 64
SEL_TOPK = 8
N_LOCAL_BLOCKS = 2
WINDOW = 512
GLA_HEADS = 4
GLA_KEY_DIM = 128
GLA_VAL_DIM = 256
GATE_RANK = 16
GATE_TAU = 16.0
GLA_CHUNK = 64
GLA_SUB = 16
N_EXPERT_GROUPS = 4
EXPERTS_PER_GROUP = 8
N_EXPERTS = N_EXPERT_GROUPS * EXPERTS_PER_GROUP
EPS = 1e-6
MASK_VALUE = -1e30

NSA_Q_WIDTH = NSA_HEADS * HEAD_DIM
NSA_KV_WIDTH = NSA_KV_GROUPS * HEAD_DIM
GLA_QK_WIDTH = GLA_HEADS * GLA_KEY_DIM
GLA_V_WIDTH = GLA_HEADS * GLA_VAL_DIM

LANES = 128
VMEM_LIMIT_BYTES = 56 * 1024 * 1024

COL_NSA_Q = 0
COL_KV = COL_NSA_Q + NSA_Q_WIDTH
COL_GLA_Q = COL_KV + 6 * NSA_KV_WIDTH
COL_GLA_K = COL_GLA_Q + GLA_QK_WIDTH
COL_GLA_V = COL_GLA_K + GLA_QK_WIDTH
COL_GLA_GATE = COL_GLA_V + GLA_V_WIDTH
BIG_WIDTH = COL_GLA_GATE + GLA_V_WIDTH
SMALL_GATE = 0
SMALL_ALPHA = 3 * NSA_HEADS
SMALL_WIDTH = LANES


def _params(semantics):
    return pltpu.CompilerParams(dimension_semantics=semantics, vmem_limit_bytes=VMEM_LIMIT_BYTES)


def _split3(a):
    hi = a.astype(jnp.bfloat16)
    r1 = a - hi.astype(jnp.float32)
    mid = r1.astype(jnp.bfloat16)
    lo = (r1 - mid.astype(jnp.float32)).astype(jnp.bfloat16)
    return hi, mid, lo


def _in_proj_kernel(x_ref, g_ref, wb_ref, ws_ref, scale_ref, sbias_ref, big_ref, small_ref, h_ref):
    n = pl.program_id(1)

    @pl.when(n == 0)
    def _():
        xf = x_ref[...]
        y = xf * lax.rsqrt(jnp.mean(xf * xf, axis=-1, keepdims=True) + EPS)
        h = (y * g_ref[...]).astype(jnp.bfloat16)
        h_ref[...] = h
        small_ref[...] = jnp.dot(h, ws_ref[...], preferred_element_type=jnp.float32) + sbias_ref[...]

    acc = jnp.dot(h_ref[...], wb_ref[...], preferred_element_type=jnp.float32)
    big_ref[...] = (acc * scale_ref[...]).astype(big_ref.dtype)


def _in_proj(x2, gain, w_big, w_small, col_scale, small_bias, tm=1024, tn=512):
    t, d = x2.shape
    tm = min(tm, t)
    grid = (t // tm, BIG_WIDTH // tn)
    return pl.pallas_call(
        _in_proj_kernel,
        grid=grid,
        in_specs=[
            pl.BlockSpec((tm, d), lambda m, n: (m, 0)),
            pl.BlockSpec((1, d), lambda m, n: (0, 0)),
            pl.BlockSpec((d, tn), lambda m, n: (0, n)),
            pl.BlockSpec((d, SMALL_WIDTH), lambda m, n: (0, 0)),
            pl.BlockSpec((1, tn), lambda m, n: (0, n)),
            pl.BlockSpec((1, SMALL_WIDTH), lambda m, n: (0, 0)),
        ],
        out_specs=[
            pl.BlockSpec((tm, tn), lambda m, n: (m, n)),
            pl.BlockSpec((tm, SMALL_WIDTH), lambda m, n: (m, 0)),
        ],
        out_shape=[
            jax.ShapeDtypeStruct((t, BIG_WIDTH), jnp.bfloat16),
            jax.ShapeDtypeStruct((t, SMALL_WIDTH), jnp.float32),
        ],
        scratch_shapes=[pltpu.VMEM((tm, d), jnp.bfloat16)],
        compiler_params=_params(("parallel", "arbitrary")),
    )(x2, gain, w_big, w_small, col_scale, small_bias)


def _prep_in_proj_weights(w_in, b_nsa_gate):
    sizes = (NSA_Q_WIDTH,) + (NSA_KV_WIDTH,) * 6 + (3 * NSA_HEADS, GLA_QK_WIDTH, GLA_QK_WIDTH,
                                                    GLA_V_WIDTH, GATE_RANK, GLA_V_WIDTH)
    offs = np.concatenate([[0], np.cumsum(sizes)])
    part = lambda i: w_in[:, offs[i]:offs[i + 1]]
    w_big = jnp.concatenate([part(i) for i in (0, 1, 2, 3, 4, 5, 6, 8, 9, 10, 12)], axis=1)
    pad = jnp.zeros((w_in.shape[0], SMALL_WIDTH - 3 * NSA_HEADS - GATE_RANK), w_in.dtype)
    w_small = jnp.concatenate([part(7), part(11), pad], axis=1)
    scale = np.ones((1, BIG_WIDTH), np.float32)
    scale[0, COL_NSA_Q:COL_NSA_Q + NSA_Q_WIDTH] = HEAD_DIM ** -0.5
    scale[0, COL_GLA_Q:COL_GLA_Q + GLA_QK_WIDTH] = GLA_KEY_DIM ** -0.5
    small_bias = jnp.concatenate([b_nsa_gate.astype(jnp.float32),
                                  jnp.zeros((SMALL_WIDTH - 3 * NSA_HEADS,), jnp.float32)])[None]
    return w_big.astype(jnp.bfloat16), w_small.astype(jnp.bfloat16), jnp.asarray(scale), small_bias


def _gelu_tanh(a):
    return 0.5 * a * (1.0 + jnp.tanh(np.sqrt(2.0 / np.pi) * (a + 0.044715 * (a * a * a))))


def _compress_kernel(z_ref, pos_ref, w1_ref, w2_ref, o_ref):
    z = z_ref[0].astype(jnp.float32)
    pos = pos_ref[0]
    z_lo = (z + pos[0:1]).astype(jnp.bfloat16)
    z_hi = (z + pos[1:2]).astype(jnp.bfloat16)
    u = jnp.dot(z_lo, w1_ref[0, 0], preferred_element_type=jnp.float32)
    v = jnp.dot(z_hi, w1_ref[0, 1], preferred_element_type=jnp.float32)
    a = u + pltpu.roll(v, v.shape[0] - 1, axis=0)
    hid = _gelu_tanh(a).astype(jnp.bfloat16)
    o_ref[0] = jnp.dot(hid, w2_ref[0], preferred_element_type=jnp.float32).astype(o_ref.dtype)


def _compress(z, pos, w1, w2, tr):
    _, rows, width = z.shape
    hid = w1.shape[-1]
    return pl.pallas_call(
        _compress_kernel,
        grid=(2, rows // tr),
        in_specs=[
            pl.BlockSpec((1, tr, width), lambda c, r: (c, r, 0)),
            pl.BlockSpec((1, 2, width), lambda c, r: (c, 0, 0)),
            pl.BlockSpec((1, 2, width, hid), lambda c, r: (c, 0, 0, 0)),
            pl.BlockSpec((1, hid, HEAD_DIM), lambda c, r: (c, 0, 0)),
        ],
        out_specs=pl.BlockSpec((1, tr, HEAD_DIM), lambda c, r: (c, r, 0)),
        out_shape=jax.ShapeDtypeStruct((2, rows, HEAD_DIM), jnp.bfloat16),
        compiler_params=_params(("parallel", "parallel")),
    )(z, pos, w1, w2)


def _compress_stage(big, b, s, cmp_pos_k, w_cmp_k1, w_cmp_k2, cmp_pos_v, w_cmp_v1, w_cmp_v2):
    g, dk = NSA_KV_GROUPS, HEAD_DIM
    n_seg = s // CMP_STRIDE
    kv = big[:, COL_KV:COL_KV + 2 * NSA_KV_WIDTH].reshape(b, n_seg, CMP_STRIDE, 2, g, dk)
    z = kv.transpose(3, 0, 4, 1, 2, 5).reshape(2, b * g * n_seg, CMP_STRIDE * dk)
    pos = jnp.stack([cmp_pos_k, cmp_pos_v]).reshape(2, 2, CMP_STRIDE * dk)
    w1 = jnp.stack([w_cmp_k1, w_cmp_v1]).reshape(2, 2, CMP_STRIDE * dk, -1).astype(jnp.bfloat16)
    w2 = jnp.stack([w_cmp_k2, w_cmp_v2]).astype(jnp.bfloat16)
    pairs = b * g
    per_tile = max(p for p in (1, 2, 4, 8) if pairs % p == 0 and p * n_seg <= 1024)
    out = _compress(z, pos, w1, w2, per_tile * n_seg)
    return out.reshape(2, b * g, n_seg, dk)


_NT = (((1,), (1,)), ((), ()))


def _nsa_kernel(slopes_ref, q_ref, kc_ref, vct_ref, ksel_ref, vselt_ref, kwin_ref, vwint_ref,
                gate_ref, ovl_ref, o_ref, selb_ref, *, tq, tk):
    g = pl.program_id(1)
    qi = pl.program_id(2)
    q0 = qi * tq
    n_seg = kc_ref.shape[1]
    n_sel = ovl_ref.shape[0]
    mxu_dtype = ksel_ref.dtype
    f32 = jnp.float32
    hd = HEAD_DIM

    t_row = q0 + lax.broadcasted_iota(jnp.int32, (1, tq), 1)

    cmp_end = lax.broadcasted_iota(jnp.int32, (n_seg, 1), 0) * CMP_STRIDE + (CMP_BLOCK - 1)
    valid_c = cmp_end <= t_row
    dist_c = (t_row - cmp_end).astype(f32)
    any_c = (t_row >= CMP_BLOCK - 1).astype(f32)
    kc = kc_ref[0]
    vct = vct_ref[0]
    psum = jnp.zeros((n_seg, tq), f32)
    o_cmp = []
    for hh in range(HEADS_PER_GROUP):
        slope = slopes_ref[g, hh]
        qh = q_ref[0, :, hh * hd:(hh + 1) * hd]
        sc = lax.dot_general(kc, qh, _NT, preferred_element_type=f32)
        sc = jnp.where(valid_c, sc - slope * dist_c, MASK_VALUE)
        m = jnp.max(sc, axis=0, keepdims=True)
        p = jnp.exp(sc - m)
        p = p * (any_c / jnp.sum(p, axis=0, keepdims=True))
        psum = psum + p
        o_cmp.append(jnp.dot(vct, p.astype(mxu_dtype), preferred_element_type=f32))

    ovl = ovl_ref[...]
    imp = jnp.zeros((n_sel, tq), f32)
    for piece in _split3(psum):
        imp = imp + jnp.dot(ovl, piece, preferred_element_type=f32)
    blk = lax.broadcasted_iota(jnp.int32, (n_sel, 1), 0)
    cur = t_row // SEL_BLOCK
    causal_blk = blk * SEL_BLOCK <= t_row
    forced = (blk == 0) | ((blk <= cur) & (blk > cur - N_LOCAL_BLOCKS))
    imp = jnp.where(forced, -MASK_VALUE, jnp.where(causal_blk, imp, MASK_VALUE))
    cnt = jnp.zeros((n_sel, tq), f32)
    for i in range(n_sel):
        ri = imp[i:i + 1, :]
        beats = jnp.where(blk > i, (ri >= imp).astype(f32), (ri > imp).astype(f32))
        cnt = cnt + beats
    topk = min(SEL_TOPK, n_sel)
    selb_ref[...] = jnp.where(cnt < topk, 0.0, MASK_VALUE)

    d0 = (lax.broadcasted_iota(jnp.int32, (tk, tq), 1)
          - lax.broadcasted_iota(jnp.int32, (tk, tq), 0))
    blocks_per_chunk = tk // SEL_BLOCK

    def flash(k_ref, vt_ref, qh, slope, c_lo, c_hi, window):
        def body(c, carry):
            m, l, acc = carry
            k0 = pl.multiple_of(c * tk, tk)
            kblk = k_ref[0, pl.ds(k0, tk), :]
            s = lax.dot_general(kblk, qh, _NT, preferred_element_type=f32)
            dist = d0 + (q0 - k0)
            s = s - slope * dist.astype(f32)
            if window:
                s = jnp.where(dist >= 0, jnp.where(dist < WINDOW, s, MASK_VALUE), MASK_VALUE)
            else:
                s = jnp.where(dist >= 0, s, MASK_VALUE)
                rows = [jnp.broadcast_to(selb_ref[pl.ds(c * blocks_per_chunk + j, 1), :],
                                         (SEL_BLOCK, tq)) for j in range(blocks_per_chunk)]
                s = s + jnp.concatenate(rows, axis=0)
            m_new = jnp.maximum(m, jnp.max(s, axis=0, keepdims=True))
            alpha = jnp.exp(m - m_new)
            p = jnp.exp(s - m_new)
            l = alpha * l + jnp.sum(p, axis=0, keepdims=True)
            vt = vt_ref[0, :, pl.ds(k0, tk)]
            acc = alpha * acc + jnp.dot(vt, p.astype(mxu_dtype), preferred_element_type=f32)
            return m_new, l, acc

        init = (jnp.full((1, tq), MASK_VALUE, f32), jnp.zeros((1, tq), f32), jnp.zeros((hd, tq), f32))
        _, l, acc = lax.fori_loop(c_lo, c_hi, body, init)
        return acc / l

    c_hi = (q0 + tq) // tk
    c_win = jnp.maximum(0, (q0 - WINDOW + 1) // tk)
    for hh in range(HEADS_PER_GROUP):
        slope = slopes_ref[g, hh]
        qh = q_ref[0, :, hh * hd:(hh + 1) * hd]
        o_sel = flash(ksel_ref, vselt_ref, qh, slope, 0, c_hi, False)
        o_win = flash(kwin_ref, vwint_ref, qh, slope, c_win, c_hi, True)
        gates = jax.nn.sigmoid(gate_ref[0, 0, 3 * hh:3 * hh + 3, :])
        out_t = gates[0:1] * o_cmp[hh] + gates[1:2] * o_sel + gates[2:3] * o_win
        o_ref[0, :, hh * hd:(hh + 1) * hd] = out_t.T.astype(o_ref.dtype)


def _nsa_stage(big, small, cmp, b, s, tq=256):
    g, hd = NSA_KV_GROUPS, HEAD_DIM
    tq = min(tq, s)
    tk = tq
    n_seg = s // CMP_STRIDE
    n_sel = s // SEL_BLOCK
    big3 = big.reshape(b, s, BIG_WIDTH)
    kv_col = lambda which: (COL_KV + which * NSA_KV_WIDTH) // hd
    vt = lambda which: big3[:, :, COL_KV + which * NSA_KV_WIDTH:COL_KV + (which + 1) * NSA_KV_WIDTH] \
        .reshape(b, s, g, hd).transpose(0, 2, 3, 1).reshape(b * g, hd, s)
    vsel_t, vwin_t = vt(3), vt(5)
    kc = cmp[0]
    vc_t = cmp[1].transpose(0, 2, 1)
    gate_t = small[:, SMALL_GATE:SMALL_GATE + 3 * NSA_HEADS].reshape(b, s, g, 3 * HEADS_PER_GROUP) \
        .transpose(0, 2, 3, 1)
    c_start = np.arange(n_seg)[None, :] * CMP_STRIDE
    s_start = np.arange(n_sel)[:, None] * SEL_BLOCK
    n_cmp = (s - CMP_BLOCK) // CMP_STRIDE + 1
    ovl = ((c_start < s_start + SEL_BLOCK) & (c_start + CMP_BLOCK > s_start)
           & (np.arange(n_seg)[None, :] < n_cmp)).astype(np.float32)
    ovl = jnp.asarray(ovl, jnp.bfloat16)
    slopes = jnp.asarray((2.0 ** (-8.0 * np.arange(1, NSA_HEADS + 1) / NSA_HEADS))
                         .reshape(g, HEADS_PER_GROUP), jnp.float32)
    gw = 3 * HEADS_PER_GROUP
    kernel = functools.partial(_nsa_kernel, tq=tq, tk=tk)
    return pl.pallas_call(
        kernel,
        grid=(b, g, s // tq),
        in_specs=[
            pl.BlockSpec(memory_space=pltpu.SMEM),
            pl.BlockSpec((1, tq, HEADS_PER_GROUP * hd), lambda bi, gi, qi: (bi, qi, gi)),
            pl.BlockSpec((1, n_seg, hd), lambda bi, gi, qi: (bi * NSA_KV_GROUPS + gi, 0, 0)),
            pl.BlockSpec((1, hd, n_seg), lambda bi, gi, qi: (bi * NSA_KV_GROUPS + gi, 0, 0)),
            pl.BlockSpec((1, s, hd), lambda bi, gi, qi: (bi, 0, kv_col(2) + gi)),
            pl.BlockSpec((1, hd, s), lambda bi, gi, qi: (bi * NSA_KV_GROUPS + gi, 0, 0)),
            pl.BlockSpec((1, s, hd), lambda bi, gi, qi: (bi, 0, kv_col(4) + gi)),
            pl.BlockSpec((1, hd, s), lambda bi, gi, qi: (bi * NSA_KV_GROUPS + gi, 0, 0)),
            pl.BlockSpec((1, 1, gw, tq), lambda bi, gi, qi: (bi, gi, 0, qi)),
            pl.BlockSpec((n_sel, n_seg), lambda bi, gi, qi: (0, 0)),
        ],
        out_specs=pl.BlockSpec((1, tq, HEADS_PER_GROUP * hd), lambda bi, gi, qi: (bi, qi, gi)),
        out_shape=jax.ShapeDtypeStruct((b, s, NSA_Q_WIDTH), big.dtype),
        scratch_shapes=[pltpu.VMEM((n_sel, tq), jnp.float32)],
        compiler_params=_params(("parallel", "parallel", "arbitrary")),
    )(slopes, big3, kc, vc_t, big3, vsel_t, big3, vwin_t, gate_t, ovl)


_TN = (((0,), (0,)), ((), ()))
_HI = lax.Precision.HIGHEST


def _gla_kernel(q_ref, k_ref, v_ref, og_ref, sm_ref, wa_ref, ba_ref, gn_ref, selw_ref, o_ref,
                state_ref, *, n_chunks):
    f32 = jnp.float32
    mxu_dtype = v_ref.dtype
    c_len, sub = GLA_CHUNK, GLA_SUB
    ns = c_len // sub
    dk = GLA_KEY_DIM

    @pl.when(pl.program_id(2) == 0)
    def _():
        state_ref[...] = jnp.zeros_like(state_ref)

    row = lax.broadcasted_iota(jnp.int32, (c_len, 1), 0)
    col = lax.broadcasted_iota(jnp.int32, (1, c_len), 1)
    tril = (col <= row).astype(f32)
    sub_row = row // sub
    sub_col = col // sub
    t_loc = lax.broadcasted_iota(jnp.int32, (1, sub, 1), 1)
    neg_inf = -jnp.inf

    for c in range(n_chunks):
        rows = slice(c * c_len, (c + 1) * c_len)
        q = q_ref[0, rows, :].astype(f32)
        k = k_ref[0, rows, :].astype(f32)
        v = v_ref[0, rows, :]
        logits = jnp.dot(sm_ref[0, rows, :], wa_ref[0], precision=_HI,
                         preferred_element_type=f32) + ba_ref[0]
        glog = (jnp.minimum(logits, 0.0) - jnp.log(1.0 + jnp.exp(-jnp.abs(logits)))) * (1.0 / GATE_TAU)
        bcum = jnp.dot(tril, glog, precision=_HI, preferred_element_type=f32)
        b_last = bcum[c_len - 1:c_len, :]
        st = state_ref[...]

        q_in = (q * jnp.exp(bcum)).astype(mxu_dtype)
        o = lax.dot_general(q_in, st.astype(mxu_dtype), _NT, preferred_element_type=f32)

        r = [bcum[(j + 1) * sub - 1:(j + 1) * sub, :] for j in range(ns)]
        r_rows = jnp.concatenate([jnp.broadcast_to(rj, (sub, dk)) for rj in r], axis=0)
        kf = (k * jnp.exp(r_rows - bcum)).astype(mxu_dtype)
        a = jnp.zeros((c_len, c_len), f32)
        for j in range(ns - 1):
            arg = jnp.where(row >= (j + 1) * sub, bcum - r[j], neg_inf)
            qf = (q * jnp.exp(arg)).astype(mxu_dtype)
            aj = lax.dot_general(qf, kf, _NT, preferred_element_type=f32)
            a = a + jnp.where(sub_col == j, aj, 0.0)

        q3 = q.reshape(ns, sub, dk)
        k3 = k.reshape(ns, sub, dk)
        b3 = bcum.reshape(ns, sub, dk)
        pieces = []
        for s_ in range(sub):
            arg = jnp.where(t_loc >= s_, b3 - b3[:, s_:s_ + 1, :], neg_inf)
            e = q3 * k3[:, s_:s_ + 1, :] * jnp.exp(arg)
            pieces.append(e.reshape(c_len, dk).astype(mxu_dtype))
        d_wide = jnp.dot(jnp.concatenate(pieces, axis=1), selw_ref[...], preferred_element_type=f32)
        a = a + jnp.where(sub_row == sub_col, d_wide, 0.0)
        o = o + jnp.dot(a.astype(mxu_dtype), v, preferred_element_type=f32)

        kd = (k * jnp.exp(b_last - bcum)).astype(mxu_dtype)
        state_ref[...] = st * jnp.exp(b_last) + lax.dot_general(v, kd, _TN, preferred_element_type=f32)

        rms = lax.rsqrt(jnp.mean(o * o, axis=-1, keepdims=True) + EPS)
        gate = og_ref[0, rows, :].astype(f32)
        y = o * rms * gn_ref[...] * (gate * jax.nn.sigmoid(gate))
        o_ref[0, rows, :] = y.astype(o_ref.dtype)


def _gla_stage(big, small, w_alpha2, b_alpha, g_norm, b, s, tc=256):
    h, dk, dv = GLA_HEADS, GLA_KEY_DIM, GLA_VAL_DIM
    tc = min(tc, s)
    big3 = big.reshape(b, s, BIG_WIDTH)
    small3 = small.reshape(b, s, SMALL_WIDTH)
    wa = jnp.zeros((h, SMALL_WIDTH, dk), jnp.float32).at[:, SMALL_ALPHA:SMALL_ALPHA + GATE_RANK, :].set(
        w_alpha2.astype(jnp.float32).reshape(GATE_RANK, h, dk).transpose(1, 0, 2))
    ba = b_alpha.astype(jnp.float32).reshape(h, 1, dk)
    gn = g_norm.astype(jnp.float32).reshape(1, dv)
    sel = (np.arange(GLA_SUB * dk)[:, None] // dk == np.arange(GLA_CHUNK)[None, :] % GLA_SUB)
    selw = jnp.asarray(sel.astype(np.float32), big.dtype)
    kernel = functools.partial(_gla_kernel, n_chunks=tc // GLA_CHUNK)
    return pl.pallas_call(
        kernel,
        grid=(b, h, s // tc),
        in_specs=[
            pl.BlockSpec((1, tc, dk), lambda bi, hi, ci: (bi, ci, COL_GLA_Q // dk + hi)),
            pl.BlockSpec((1, tc, dk), lambda bi, hi, ci: (bi, ci, COL_GLA_K // dk + hi)),
            pl.BlockSpec((1, tc, dv), lambda bi, hi, ci: (bi, ci, COL_GLA_V // dv + hi)),
            pl.BlockSpec((1, tc, dv), lambda bi, hi, ci: (bi, ci, COL_GLA_GATE // dv + hi)),
            pl.BlockSpec((1, tc, SMALL_WIDTH), lambda bi, hi, ci: (bi, ci, 0)),
            pl.BlockSpec((1, SMALL_WIDTH, dk), lambda bi, hi, ci: (hi, 0, 0)),
            pl.BlockSpec((1, 1, dk), lambda bi, hi, ci: (hi, 0, 0)),
            pl.BlockSpec((1, dv), lambda bi, hi, ci: (0, 0)),
            pl.BlockSpec((GLA_SUB * dk, GLA_CHUNK), lambda bi, hi, ci: (0, 0)),
        ],
        out_specs=pl.BlockSpec((1, tc, dv), lambda bi, hi, ci: (bi, ci, hi)),
        out_shape=jax.ShapeDtypeStruct((b, s, GLA_V_WIDTH), big.dtype),
        scratch_shapes=[pltpu.VMEM((dv, dk), jnp.float32)],
        compiler_params=_params(("parallel", "parallel", "arbitrary")),
    )(big3, big3, big3, big3, small3, wa, ba, gn, selw)


ROUTE_E0 = N_EXPERT_GROUPS
ROUTE_ID = ROUTE_E0 + N_EXPERTS
ROUTE_W = ROUTE_ID + 2
MOE_TM = 256


def _out_proj_kernel(x_ref, nsa_ref, gla_ref, wt_ref, wb_ref, g_ref, wr_ref, br_ref,
                     x1_ref, h_ref, comb_ref):
    f32 = jnp.float32
    acc = jnp.dot(nsa_ref[...], wt_ref[...], preferred_element_type=f32)
    acc = acc + jnp.dot(gla_ref[...], wb_ref[...], preferred_element_type=f32)
    x1 = x_ref[...] + acc
    x1_ref[...] = x1
    hf = x1 * lax.rsqrt(jnp.mean(x1 * x1, axis=-1, keepdims=True) + EPS) * g_ref[...]
    h_ref[...] = hf.astype(h_ref.dtype)

    h_hi, h_mid, _ = _split3(hf)
    logits = (jnp.dot(h_hi, wr_ref[0], preferred_element_type=f32)
              + jnp.dot(h_hi, wr_ref[1], preferred_element_type=f32)
              + jnp.dot(h_mid, wr_ref[0], preferred_element_type=f32)) + br_ref[...]

    lane = lax.broadcasted_iota(jnp.int32, (1, LANES), 1).astype(f32)
    big_lane = float(LANES)
    neg_inf = -jnp.inf
    first_argmax = lambda vals, vmax: jnp.min(jnp.where(vals == vmax, lane, big_lane), axis=-1, keepdims=True)
    gl = jnp.where(lane < N_EXPERT_GROUPS, logits, neg_inf)
    gmax = jnp.max(gl, axis=-1, keepdims=True)
    gsel = first_argmax(gl, gmax)
    gw = 1.0 / jnp.sum(jnp.exp(gl - gmax), axis=-1, keepdims=True)
    lo = ROUTE_E0 + EXPERTS_PER_GROUP * gsel
    el = jnp.where(lane >= lo, jnp.where(lane < lo + EXPERTS_PER_GROUP, logits, neg_inf), neg_inf)
    m1 = jnp.max(el, axis=-1, keepdims=True)
    i1 = first_argmax(el, m1)
    el2 = jnp.where(lane == i1, neg_inf, el)
    m2 = jnp.max(el2, axis=-1, keepdims=True)
    i2 = first_argmax(el2, m2)
    e2 = jnp.exp(m2 - m1)
    w1 = gw / (1.0 + e2)
    w2 = w1 * e2
    onehot = jnp.where(lane == i1, 1.0, 0.0) + jnp.where(lane == i2, 1.0, 0.0)
    meta = (jnp.where(lane == ROUTE_ID, i1 - ROUTE_E0, 0.0) + jnp.where(lane == ROUTE_ID + 1, i2 - ROUTE_E0, 0.0)
            + jnp.where(lane == ROUTE_W, w1, 0.0) + jnp.where(lane == ROUTE_W + 1, w2, 0.0))
    comb_ref[...] = onehot + meta


def _out_proj_stage(x2, nsa, gla, w_out, g_ffn, w_rg, b_rg, w_re, b_re, tm=512):
    t, d = x2.shape
    tm = min(tm, t)
    half = nsa.shape[1]
    w_top = w_out[:half].astype(nsa.dtype)
    w_bot = w_out[half:].astype(nsa.dtype)
    wr = jnp.concatenate([w_rg, w_re.reshape(d, N_EXPERTS),
                          jnp.zeros((d, LANES - ROUTE_E0 - N_EXPERTS), jnp.float32)], axis=1)
    wr_hi, wr_mid, _ = _split3(wr)
    wr2 = jnp.stack([wr_hi, wr_mid])
    br = jnp.concatenate([b_rg, b_re.reshape(N_EXPERTS),
                          jnp.zeros((LANES - ROUTE_E0 - N_EXPERTS,), jnp.float32)])[None]
    row = lambda m: (m, 0)
    fixed = lambda m: (0, 0)
    return pl.pallas_call(
        _out_proj_kernel,
        grid=(t // tm,),
        in_specs=[
            pl.BlockSpec((tm, d), row),
            pl.BlockSpec((tm, half), row),
            pl.BlockSpec((tm, half), row),
            pl.BlockSpec((half, d), fixed),
            pl.BlockSpec((half, d), fixed),
            pl.BlockSpec((1, d), fixed),
            pl.BlockSpec((2, d, LANES), lambda m: (0, 0, 0)),
            pl.BlockSpec((1, LANES), fixed),
        ],
        out_specs=[pl.BlockSpec((tm, d), row), pl.BlockSpec((tm, d), row), pl.BlockSpec((tm, LANES), row)],
        out_shape=[
            jax.ShapeDtypeStruct((t, d), jnp.float32),
            jax.ShapeDtypeStruct((t, d), jnp.float32),
            jax.ShapeDtypeStruct((t, LANES), jnp.float32),
        ],
        compiler_params=_params(("parallel",)),
    )(x2, nsa, gla, w_top, w_bot, g_ffn.astype(jnp.float32)[None], wr2, br)


def _rank_kernel(route_ref, rank_ref, count_ref, carry_ref):
    f32 = jnp.float32
    tm = route_ref.shape[0]

    @pl.when(pl.program_id(0) == 0)
    def _():
        carry_ref[...] = jnp.zeros_like(carry_ref)

    route = route_ref[...]
    lane = lax.broadcasted_iota(jnp.int32, (1, LANES), 1)
    lane_f = lane.astype(f32)
    onehot = jnp.where((lane >= ROUTE_E0) & (lane < ROUTE_ID), route, 0.0)
    row = lax.broadcasted_iota(jnp.int32, (tm, tm), 0)
    col = lax.broadcasted_iota(jnp.int32, (tm, tm), 1)
    strict_lower = (col < row).astype(jnp.bfloat16)
    before = jnp.dot(strict_lower, onehot.astype(jnp.bfloat16), preferred_element_type=f32) + carry_ref[0:1, :]
    ranks = []
    for j in range(2):
        lane_j = route[:, ROUTE_ID + j:ROUTE_ID + j + 1] + float(ROUTE_E0)
        ranks.append(jnp.sum(jnp.where(lane_f == lane_j, before, 0.0), axis=-1, keepdims=True))
    rank_ref[...] = jnp.where(lane == 0, ranks[0], jnp.where(lane == 1, ranks[1], 0.0))
    total = carry_ref[0:1, :] + jnp.sum(onehot, axis=0, keepdims=True)
    carry_ref[...] = jnp.broadcast_to(total, carry_ref.shape)
    count_ref[...] = jnp.broadcast_to(total, count_ref.shape)


def _rank_stage(route, tm=512):
    t = route.shape[0]
    tm = min(tm, t)
    return pl.pallas_call(
        _rank_kernel,
        grid=(t // tm,),
        in_specs=[pl.BlockSpec((tm, LANES), lambda m: (m, 0))],
        out_specs=[pl.BlockSpec((tm, LANES), lambda m: (m, 0)), pl.BlockSpec((8, LANES), lambda m: (0, 0))],
        out_shape=[jax.ShapeDtypeStruct((t, LANES), jnp.float32), jax.ShapeDtypeStruct((8, LANES), jnp.float32)],
        scratch_shapes=[pltpu.VMEM((8, LANES), jnp.float32)],
        compiler_params=_params(("arbitrary",)),
    )(route)


def _gather_rows(idx_ref, n_rows, src_hbm, dst, sem):
    def body(r, carry):
        tok = idx_ref[0, 0, r]
        pltpu.make_async_copy(src_hbm.at[pl.ds(tok, 1), :], dst.at[pl.ds(r, 1), :], sem).start()
        return carry
    lax.fori_loop(0, n_rows, body, 0)


def _wait_rows(n_rows, src_hbm, dst, sem):
    pltpu.make_async_copy(src_hbm.at[pl.ds(0, n_rows), :], dst, sem).wait()


def _moe_kernel(te_ref, na_ref, idx_ref, idx_next_ref, h_hbm, wg_ref, wu_ref, wd_ref, y_ref, buf, sem):
    f32 = jnp.float32
    i = pl.program_id(0)
    n_active = na_ref[0]
    tm = buf.shape[1]

    @pl.when((i == 0) & (n_active > 0))
    def _():
        _gather_rows(idx_ref, tm, h_hbm, buf.at[0], sem.at[0])

    @pl.when(i + 1 < n_active)
    def _():
        nxt = (i + 1) % 2
        _gather_rows(idx_next_ref, tm, h_hbm, buf.at[nxt], sem.at[nxt])

    @pl.when(i < n_active)
    def _():
        slot = i % 2
        _wait_rows(tm, h_hbm, buf.at[slot], sem.at[slot])
        xb = buf[slot].astype(jnp.bfloat16)
        gate = jnp.dot(xb, wg_ref[0], preferred_element_type=f32)
        up = jnp.dot(xb, wu_ref[0], preferred_element_type=f32)
        hid = (gate * jax.nn.sigmoid(gate) * up).astype(jnp.bfloat16)
        y_ref[...] = jnp.dot(hid, wd_ref[0], preferred_element_type=f32)

    @pl.when(i >= n_active)
    def _():
        y_ref[...] = jnp.zeros_like(y_ref)


def _moe_stage(h2, tok_of_pos, tile_expert, n_active, w_gate, w_up, w_down):
    t, d = h2.shape
    n_tiles, _, tm = tok_of_pos.shape
    f = w_gate.shape[-1]
    wg = w_gate.reshape(N_EXPERTS, d, f).astype(jnp.bfloat16)
    wu = w_up.reshape(N_EXPERTS, d, f).astype(jnp.bfloat16)
    wd = w_down.reshape(N_EXPERTS, f, d).astype(jnp.bfloat16)
    grid_spec = pltpu.PrefetchScalarGridSpec(
        num_scalar_prefetch=2,
        grid=(n_tiles,),
        in_specs=[
            pl.BlockSpec((1, 1, tm), lambda i, te, na: (i, 0, 0), memory_space=pltpu.SMEM),
            pl.BlockSpec((1, 1, tm), lambda i, te, na: (jnp.minimum(i + 1, n_tiles - 1), 0, 0),
                         memory_space=pltpu.SMEM),
            pl.BlockSpec(memory_space=pl.ANY),
            pl.BlockSpec((1, d, f), lambda i, te, na: (te[i], 0, 0)),
            pl.BlockSpec((1, d, f), lambda i, te, na: (te[i], 0, 0)),
            pl.BlockSpec((1, f, d), lambda i, te, na: (te[i], 0, 0)),
        ],
        out_specs=pl.BlockSpec((tm, d), lambda i, te, na: (i, 0)),
        scratch_shapes=[pltpu.VMEM((2, tm, d), jnp.float32), pltpu.SemaphoreType.DMA((2,))],
    )
    return pl.pallas_call(
        _moe_kernel,
        grid_spec=grid_spec,
        out_shape=jax.ShapeDtypeStruct((n_tiles * tm, d), jnp.float32),
        compiler_params=_params(("arbitrary",)),
    )(tile_expert, n_active, tok_of_pos, tok_of_pos, h2, wg, wu, wd)


def _combine_kernel(p1_ref, p2_ref, p1n_ref, p2n_ref, route_ref, x1_ref, gf_ref, y_hbm, o_ref, buf, sem,
                    *, final_norm):
    i = pl.program_id(0)
    n = pl.num_programs(0)
    tm = buf.shape[2]

    def gather(a_ref, b_ref, slot):
        _gather_rows(a_ref, tm, y_hbm, buf.at[slot, 0], sem.at[slot, 0])
        _gather_rows(b_ref, tm, y_hbm, buf.at[slot, 1], sem.at[slot, 1])

    @pl.when(i == 0)
    def _():
        gather(p1_ref, p2_ref, 0)

    @pl.when(i + 1 < n)
    def _():
        gather(p1n_ref, p2n_ref, (i + 1) % 2)

    slot = i % 2
    _wait_rows(tm, y_hbm, buf.at[slot, 0], sem.at[slot, 0])
    _wait_rows(tm, y_hbm, buf.at[slot, 1], sem.at[slot, 1])
    route = route_ref[...]
    w1 = route[:, ROUTE_W:ROUTE_W + 1]
    w2 = route[:, ROUTE_W + 1:ROUTE_W + 2]
    y = x1_ref[...] + (w1 * buf[slot, 0] + w2 * buf[slot, 1])
    if final_norm:
        y = y * lax.rsqrt(jnp.mean(y * y, axis=-1, keepdims=True) + EPS) * gf_ref[...]
    o_ref[...] = y


def _combine_stage(y_sorted, pos1, pos2, route, x1, g_final, final_norm, tm=256):
    t, d = x1.shape
    tm = min(tm, t)
    n = t // tm
    p1 = pos1.reshape(n, 1, tm)
    p2 = pos2.reshape(n, 1, tm)
    cur = lambda i: (i, 0, 0)
    nxt = lambda i: (jnp.minimum(i + 1, n - 1), 0, 0)
    smem = lambda index_map: pl.BlockSpec((1, 1, tm), index_map, memory_space=pltpu.SMEM)
    return pl.pallas_call(
        functools.partial(_combine_kernel, final_norm=final_norm),
        grid=(n,),
        in_specs=[
            smem(cur), smem(cur), smem(nxt), smem(nxt),
            pl.BlockSpec((tm, LANES), lambda i: (i, 0)),
            pl.BlockSpec((tm, d), lambda i: (i, 0)),
            pl.BlockSpec((1, d), lambda i: (0, 0)),
            pl.BlockSpec(memory_space=pl.ANY),
        ],
        out_specs=pl.BlockSpec((tm, d), lambda i: (i, 0)),
        out_shape=jax.ShapeDtypeStruct((t, d), jnp.float32),
        scratch_shapes=[pltpu.VMEM((2, 2, tm, d), jnp.float32), pltpu.SemaphoreType.DMA((2, 2))],
        compiler_params=_params(("arbitrary",)),
    )(p1, p2, p1, p2, route, x1, g_final.astype(jnp.float32)[None], y_sorted)


def _routing_tables(route, rank, count, tm):
    t = route.shape[0]
    e1 = route[:, ROUTE_ID].astype(jnp.int32)
    e2 = route[:, ROUTE_ID + 1].astype(jnp.int32)
    counts = count[0, ROUTE_E0:ROUTE_E0 + N_EXPERTS].astype(jnp.int32)
    padded = ((counts + tm - 1) // tm) * tm
    ends = jnp.cumsum(padded)
    offs = ends - padded
    pos1 = offs[e1] + rank[:, 0].astype(jnp.int32)
    pos2 = offs[e2] + rank[:, 1].astype(jnp.int32)
    n_tiles = (2 * t) // tm + N_EXPERTS
    tok = jnp.arange(t, dtype=jnp.int32)
    tok_of_pos = jnp.zeros((n_tiles * tm,), jnp.int32).at[pos1].set(tok).at[pos2].set(tok)
    tile_start = jnp.arange(n_tiles, dtype=jnp.int32) * tm
    tile_expert = jnp.minimum(jnp.searchsorted(ends, tile_start, side="right"), N_EXPERTS - 1).astype(jnp.int32)
    n_active = (ends[-1] // tm).astype(jnp.int32).reshape(1)
    return pos1, pos2, tok_of_pos.reshape(n_tiles, 1, tm), tile_expert, n_active


def kernel(x, g_mix_norm, w_in, b_nsa_gate, cmp_pos_k, w_cmp_k1, w_cmp_k2, cmp_pos_v, w_cmp_v1, w_cmp_v2,
           w_alpha2, b_alpha, g_gla_norm, w_out, g_ffn_norm, w_router_group, b_router_group,
           w_router_expert, b_router_expert, w_expert_gate, w_expert_up, w_expert_down, g_final_norm):
    b, s, d = x.shape
    depth = w_in.shape[0]
    x2 = x.reshape(b * s, d)
    for l in range(depth):
        last = l == depth - 1
        w_big, w_small, col_scale, small_bias = _prep_in_proj_weights(w_in[l], b_nsa_gate[l])
        big, small = _in_proj(x2, g_mix_norm[l].astype(jnp.float32)[None], w_big, w_small, col_scale, small_bias)
        cmp = _compress_stage(big, b, s, cmp_pos_k[l], w_cmp_k1[l], w_cmp_k2[l],
                              cmp_pos_v[l], w_cmp_v1[l], w_cmp_v2[l])
        nsa = _nsa_stage(big, small, cmp, b, s).reshape(b * s, NSA_Q_WIDTH)
        gla = _gla_stage(big, small, w_alpha2[l], b_alpha[l], g_gla_norm[l], b, s).reshape(b * s, GLA_V_WIDTH)
        x1, h2, route = _out_proj_stage(x2, nsa, gla, w_out[l], g_ffn_norm[l], w_router_group[l],
                                        b_router_group[l], w_router_expert[l], b_router_expert[l])
        rank, count = _rank_stage(route)
        pos1, pos2, tok_of_pos, tile_expert, n_active = _routing_tables(route, rank, count, MOE_TM)
        y_sorted = _moe_stage(h2, tok_of_pos, tile_expert, n_active,
                              w_expert_gate[l], w_expert_up[l], w_expert_down[l])
        x2 = _combine_stage(y_sorted, pos1, pos2, route, x1, g_final_norm, last)
    return x2.reshape(b, s, d)
```

```python
import functools

import numpy as np
import jax
import jax.numpy as jnp
from jax import lax
from jax.experimental import pallas as pl
from jax.experimental.pallas import tpu as pltpu

NSA_HEADS = 8
NSA_KV_GROUPS = 2
HEADS_PER_GROUP = NSA_HEADS // NSA_KV_GROUPS
HEAD_DIM = 128
CMP_BLOCK = 32
CMP_STRIDE = 16
SEL_BLOCK = 64
SEL_TOPK = 8
N_LOCAL_BLOCKS = 2
WINDOW = 512
GLA_HEADS = 4
GLA_KEY_DIM = 128
GLA_VAL_DIM = 256
GATE_RANK = 16
GATE_TAU = 16.0
GLA_CHUNK = 64
GLA_SUB = 16
N_EXPERT_GROUPS = 4
EXPERTS_PER_GROUP = 8
N_EXPERTS = N_EXPERT_GROUPS * EXPERTS_PER_GROUP
EPS = 1e-6
MASK_VALUE = -1e30

NSA_Q_WIDTH = NSA_HEADS * HEAD_DIM
NSA_KV_WIDTH = NSA_KV_GROUPS * HEAD_DIM
GLA_QK_WIDTH = GLA_HEADS * GLA_KEY_DIM
GLA_V_WIDTH = GLA_HEADS * GLA_VAL_DIM

LANES = 128
VMEM_LIMIT_BYTES = 56 * 1024 * 1024

COL_NSA_Q = 0
COL_KV = COL_NSA_Q + NSA_Q_WIDTH
COL_GLA_Q = COL_KV + 6 * NSA_KV_WIDTH
COL_GLA_K = COL_GLA_Q + GLA_QK_WIDTH
COL_GLA_V = COL_GLA_K + GLA_QK_WIDTH
COL_GLA_GATE = COL_GLA_V + GLA_V_WIDTH
BIG_WIDTH = COL_GLA_GATE + GLA_V_WIDTH
SMALL_GATE = 0
SMALL_ALPHA = 3 * NSA_HEADS
SMALL_WIDTH = LANES


def _params(semantics):
    return pltpu.CompilerParams(dimension_semantics=semantics, vmem_limit_bytes=VMEM_LIMIT_BYTES)


def _split3(a):
    hi = a.astype(jnp.bfloat16)
    r1 = a - hi.astype(jnp.float32)
    mid = r1.astype(jnp.bfloat16)
    lo = (r1 - mid.astype(jnp.float32)).astype(jnp.bfloat16)
    return hi, mid, lo


def _in_proj_kernel(x_ref, g_ref, wb_ref, ws_ref, scale_ref, sbias_ref, big_ref, small_ref, h_ref):
    n = pl.program_id(1)

    @pl.when(n == 0)
    def _():
        xf = x_ref[...]
        y = xf * lax.rsqrt(jnp.mean(xf * xf, axis=-1, keepdims=True) + EPS)
        h = (y * g_ref[...]).astype(jnp.bfloat16)
        h_ref[...] = h
        small_ref[...] = jnp.dot(h, ws_ref[...], preferred_element_type=jnp.float32) + sbias_ref[...]

    acc = jnp.dot(h_ref[...], wb_ref[...], preferred_element_type=jnp.float32)
    big_ref[...] = (acc * scale_ref[...]).astype(big_ref.dtype)


def _in_proj(x2, gain, w_big, w_small, col_scale, small_bias, tm=1024, tn=512):
    t, d = x2.shape
    tm = min(tm, t)
    grid = (t // tm, BIG_WIDTH // tn)
    return pl.pallas_call(
        _in_proj_kernel,
        grid=grid,
        in_specs=[
            pl.BlockSpec((tm, d), lambda m, n: (m, 0)),
            pl.BlockSpec((1, d), lambda m, n: (0, 0)),
            pl.BlockSpec((d, tn), lambda m, n: (0, n)),
            pl.BlockSpec((d, SMALL_WIDTH), lambda m, n: (0, 0)),
            pl.BlockSpec((1, tn), lambda m, n: (0, n)),
            pl.BlockSpec((1, SMALL_WIDTH), lambda m, n: (0, 0)),
        ],
        out_specs=[
            pl.BlockSpec((tm, tn), lambda m, n: (m, n)),
            pl.BlockSpec((tm, SMALL_WIDTH), lambda m, n: (m, 0)),
        ],
        out_shape=[
            jax.ShapeDtypeStruct((t, BIG_WIDTH), jnp.bfloat16),
            jax.ShapeDtypeStruct((t, SMALL_WIDTH), jnp.float32),
        ],
        scratch_shapes=[pltpu.VMEM((tm, d), jnp.bfloat16)],
        compiler_params=_params(("parallel", "arbitrary")),
    )(x2, gain, w_big, w_small, col_scale, small_bias)


def _prep_in_proj_weights(w_in, b_nsa_gate):
    sizes = (NSA_Q_WIDTH,) + (NSA_KV_WIDTH,) * 6 + (3 * NSA_HEADS, GLA_QK_WIDTH, GLA_QK_WIDTH,
                                                    GLA_V_WIDTH, GATE_RANK, GLA_V_WIDTH)
    offs = np.concatenate([[0], np.cumsum(sizes)])
    part = lambda i: w_in[:, offs[i]:offs[i + 1]]
    w_big = jnp.concatenate([part(i) for i in (0, 1, 2, 3, 4, 5, 6, 8, 9, 10, 12)], axis=1)
    pad = jnp.zeros((w_in.shape[0], SMALL_WIDTH - 3 * NSA_HEADS - GATE_RANK), w_in.dtype)
    w_small = jnp.concatenate([part(7), part(11), pad], axis=1)
    scale = np.ones((1, BIG_WIDTH), np.float32)
    scale[0, COL_NSA_Q:COL_NSA_Q + NSA_Q_WIDTH] = HEAD_DIM ** -0.5
    scale[0, COL_GLA_Q:COL_GLA_Q + GLA_QK_WIDTH] = GLA_KEY_DIM ** -0.5
    small_bias = jnp.concatenate([b_nsa_gate.astype(jnp.float32),
                                  jnp.zeros((SMALL_WIDTH - 3 * NSA_HEADS,), jnp.float32)])[None]
    return w_big.astype(jnp.bfloat16), w_small.astype(jnp.bfloat16), jnp.asarray(scale), small_bias


def _gelu_tanh(a):
    return 0.5 * a * (1.0 + jnp.tanh(np.sqrt(2.0 / np.pi) * (a + 0.044715 * (a * a * a))))


def _compress_kernel(z_ref, pos_ref, w1_ref, w2_ref, o_ref):
    z = z_ref[0].astype(jnp.float32)
    pos = pos_ref[0]
    z_lo = (z + pos[0:1]).astype(jnp.bfloat16)
    z_hi = (z + pos[1:2]).astype(jnp.bfloat16)
    u = jnp.dot(z_lo, w1_ref[0, 0], preferred_element_type=jnp.float32)
    v = jnp.dot(z_hi, w1_ref[0, 1], preferred_element_type=jnp.float32)
    a = u + pltpu.roll(v, v.shape[0] - 1, axis=0)
    hid = _gelu_tanh(a).astype(jnp.bfloat16)
    o_ref[0] = jnp.dot(hid, w2_ref[0], preferred_element_type=jnp.float32).astype(o_ref.dtype)


def _compress(z, pos, w1, w2, tr):
    _, rows, width = z.shape
    hid = w1.shape[-1]
    return pl.pallas_call(
        _compress_kernel,
        grid=(2, rows // tr),
        in_specs=[
            pl.BlockSpec((1, tr, width), lambda c, r: (c, r, 0)),
            pl.BlockSpec((1, 2, width), lambda c, r: (c, 0, 0)),
            pl.BlockSpec((1, 2, width, hid), lambda c, r: (c, 0, 0, 0)),
            pl.BlockSpec((1, hid, HEAD_DIM), lambda c, r: (c, 0, 0)),
        ],
        out_specs=pl.BlockSpec((1, tr, HEAD_DIM), lambda c, r: (c, r, 0)),
        out_shape=jax.ShapeDtypeStruct((2, rows, HEAD_DIM), jnp.bfloat16),
        compiler_params=_params(("parallel", "parallel")),
    )(z, pos, w1, w2)


def _compress_stage(big, b, s, cmp_pos_k, w_cmp_k1, w_cmp_k2, cmp_pos_v, w_cmp_v1, w_cmp_v2):
    g, dk = NSA_KV_GROUPS, HEAD_DIM
    n_seg = s // CMP_STRIDE
    kv = big[:, COL_KV:COL_KV + 2 * NSA_KV_WIDTH].reshape(b, n_seg, CMP_STRIDE, 2, g, dk)
    z = kv.transpose(3, 0, 4, 1, 2, 5).reshape(2, b * g * n_seg, CMP_STRIDE * dk)
    pos = jnp.stack([cmp_pos_k, cmp_pos_v]).reshape(2, 2, CMP_STRIDE * dk)
    w1 = jnp.stack([w_cmp_k1, w_cmp_v1]).reshape(2, 2, CMP_STRIDE * dk, -1).astype(jnp.bfloat16)
    w2 = jnp.stack([w_cmp_k2, w_cmp_v2]).astype(jnp.bfloat16)
    pairs = b * g
    per_tile = max(p for p in (1, 2, 4, 8) if pairs % p == 0 and p * n_seg <= 1024)
    out = _compress(z, pos, w1, w2, per_tile * n_seg)
    return out.reshape(2, b * g, n_seg, dk)


_NT = (((1,), (1,)), ((), ()))


def _nsa_kernel(slopes_ref, q_ref, kc_ref, vct_ref, ksel_ref, vselt_ref, kwin_ref, vwint_ref,
                gate_ref, ovl_ref, o_ref, selb_ref, alibi_ref, acc_ref, ml_ref, out_ref_t, *, tq, tk):
    g = pl.program_id(1)
    qi = pl.program_id(2)
    q0 = qi * tq
    n_seg = kc_ref.shape[1]
    n_sel = ovl_ref.shape[0]
    mxu_dtype = ksel_ref.dtype
    f32 = jnp.float32
    hd = HEAD_DIM

    t_row = q0 + lax.broadcasted_iota(jnp.int32, (1, tq), 1)

    cmp_end = lax.broadcasted_iota(jnp.int32, (n_seg, 1), 0) * CMP_STRIDE + (CMP_BLOCK - 1)
    valid_c = cmp_end <= t_row
    dist_c = (t_row - cmp_end).astype(f32)
    any_c = (t_row >= CMP_BLOCK - 1).astype(f32)
    kc = kc_ref[0]
    vct = vct_ref[0]
    psum = jnp.zeros((n_seg, tq), f32)
    heads = range(HEADS_PER_GROUP)
    slopes = [slopes_ref[g, hh] for hh in heads]
    gates = [jax.nn.sigmoid(gate_ref[0, 0, 3 * hh:3 * hh + 3, :]) for hh in heads]
    for hh in heads:
        qh = q_ref[0, :, hh * hd:(hh + 1) * hd]
        sc = lax.dot_general(kc, qh, _NT, preferred_element_type=f32)
        sc = jnp.where(valid_c, sc - slopes[hh] * dist_c, MASK_VALUE)
        m = jnp.max(sc, axis=0, keepdims=True)
        p = jnp.exp(sc - m)
        p = p * (any_c / jnp.sum(p, axis=0, keepdims=True))
        psum = psum + p
        o_cmp = jnp.dot(vct, p.astype(mxu_dtype), preferred_element_type=f32)
        out_ref_t[hh] = gates[hh][0:1] * o_cmp

    ovl = ovl_ref[...]
    imp = jnp.zeros((n_sel, tq), f32)
    for piece in _split3(psum):
        imp = imp + jnp.dot(ovl, piece, preferred_element_type=f32)
    blk = lax.broadcasted_iota(jnp.int32, (n_sel, 1), 0)
    cur = t_row // SEL_BLOCK
    causal_blk = blk * SEL_BLOCK <= t_row
    forced = (blk == 0) | ((blk <= cur) & (blk > cur - N_LOCAL_BLOCKS))
    imp = jnp.where(forced, -MASK_VALUE, jnp.where(causal_blk, imp, MASK_VALUE))
    cnt = jnp.zeros((n_sel, tq), f32)
    for i in range(n_sel):
        ri = imp[i:i + 1, :]
        beats = jnp.where(blk > i, (ri >= imp).astype(f32), (ri > imp).astype(f32))
        cnt = cnt + beats
    topk = min(SEL_TOPK, n_sel)
    selb_ref[...] = jnp.where(cnt < topk, 0.0, MASK_VALUE)

    d0 = (lax.broadcasted_iota(jnp.int32, (tk, tq), 1)
          - lax.broadcasted_iota(jnp.int32, (tk, tq), 0))
    d0f = d0.astype(f32)
    for hh in heads:
        alibi_ref[hh] = slopes[hh] * d0f
    blocks_per_chunk = tk // SEL_BLOCK

    def reset():
        acc_ref[...] = jnp.zeros_like(acc_ref)
        for hh in heads:
            ml_ref[2 * hh, 0:1, :] = jnp.full((1, tq), MASK_VALUE, f32)
            ml_ref[2 * hh + 1, 0:1, :] = jnp.zeros((1, tq), f32)

    def chunk(k_ref, vt_ref, k0, bias, delta_f):
        kblk = k_ref[0, pl.ds(k0, tk), :]
        vt = vt_ref[0, :, pl.ds(k0, tk)]
        scores = [lax.dot_general(kblk, q_ref[0, :, hh * hd:(hh + 1) * hd], _NT, preferred_element_type=f32)
                  for hh in heads]
        for hh in heads:
            s = scores[hh] - alibi_ref[hh]
            if bias is not None:
                s = s + bias
            cst = -slopes[hh] * delta_f
            m = ml_ref[2 * hh, 0:1, :]
            l = ml_ref[2 * hh + 1, 0:1, :]
            m_new = jnp.maximum(m, jnp.max(s, axis=0, keepdims=True) + cst)
            alpha = jnp.exp(m - m_new)
            p = jnp.exp(s - (m_new - cst))
            ml_ref[2 * hh, 0:1, :] = m_new
            ml_ref[2 * hh + 1, 0:1, :] = alpha * l + jnp.sum(p, axis=0, keepdims=True)
            acc_ref[hh] = alpha * acc_ref[hh] + jnp.dot(vt, p.astype(mxu_dtype), preferred_element_type=f32)

    def finish(branch):
        for hh in heads:
            out_ref_t[hh] += gates[hh][branch:branch + 1] * (acc_ref[hh] / ml_ref[2 * hh + 1, 0:1, :])

    def sel_rows(c):
        rows = [jnp.broadcast_to(selb_ref[pl.ds(c * blocks_per_chunk + j, 1), :], (SEL_BLOCK, tq))
                for j in range(blocks_per_chunk)]
        return jnp.concatenate(rows, axis=0)

    reset()

    def sel_body(c, carry):
        k0 = pl.multiple_of(c * tk, tk)
        chunk(ksel_ref, vselt_ref, k0, sel_rows(c), (q0 - k0).astype(f32))
        return carry

    lax.fori_loop(0, qi, sel_body, 0)
    k_diag = pl.multiple_of(q0, tk)
    chunk(ksel_ref, vselt_ref, k_diag, sel_rows(qi) + jnp.where(d0 >= 0, 0.0, MASK_VALUE), 0.0)
    finish(1)

    reset()
    for delta in range(0, WINDOW + tk, tk):
        lo_ok = delta - (tk - 1) >= 0
        hi_ok = delta + (tq - 1) < WINDOW
        if lo_ok and hi_ok:
            bias = None
        else:
            dist = d0 + delta
            bias = jnp.where(dist >= 0, jnp.where(dist < WINDOW, 0.0, MASK_VALUE), MASK_VALUE)
        if delta == 0:
            chunk(kwin_ref, vwint_ref, k_diag, bias, 0.0)
        else:
            @pl.when(q0 >= delta)
            def _(delta=delta, bias=bias):
                chunk(kwin_ref, vwint_ref, pl.multiple_of(q0 - delta, tk), bias, float(delta))
    finish(2)

    for hh in heads:
        o_ref[0, :, hh * hd:(hh + 1) * hd] = out_ref_t[hh].T.astype(o_ref.dtype)


def _nsa_stage(big, small, cmp, b, s, tq=256):
    g, hd = NSA_KV_GROUPS, HEAD_DIM
    tq = min(tq, s)
    tk = tq
    n_seg = s // CMP_STRIDE
    n_sel = s // SEL_BLOCK
    big3 = big.reshape(b, s, BIG_WIDTH)
    kv_col = lambda which: (COL_KV + which * NSA_KV_WIDTH) // hd
    vt = lambda which: big3[:, :, COL_KV + which * NSA_KV_WIDTH:COL_KV + (which + 1) * NSA_KV_WIDTH] \
        .reshape(b, s, g, hd).transpose(0, 2, 3, 1).reshape(b * g, hd, s)
    vsel_t, vwin_t = vt(3), vt(5)
    kc = cmp[0]
    vc_t = cmp[1].transpose(0, 2, 1)
    gate_t = small[:, SMALL_GATE:SMALL_GATE + 3 * NSA_HEADS].reshape(b, s, g, 3 * HEADS_PER_GROUP) \
        .transpose(0, 2, 3, 1)
    c_start = np.arange(n_seg)[None, :] * CMP_STRIDE
    s_start = np.arange(n_sel)[:, None] * SEL_BLOCK
    n_cmp = (s - CMP_BLOCK) // CMP_STRIDE + 1
    ovl = ((c_start < s_start + SEL_BLOCK) & (c_start + CMP_BLOCK > s_start)
           & (np.arange(n_seg)[None, :] < n_cmp)).astype(np.float32)
    ovl = jnp.asarray(ovl, jnp.bfloat16)
    slopes = jnp.asarray((2.0 ** (-8.0 * np.arange(1, NSA_HEADS + 1) / NSA_HEADS))
                         .reshape(g, HEADS_PER_GROUP), jnp.float32)
    gw = 3 * HEADS_PER_GROUP
    kernel = functools.partial(_nsa_kernel, tq=tq, tk=tk)
    return pl.pallas_call(
        kernel,
        grid=(b, g, s // tq),
        in_specs=[
            pl.BlockSpec(memory_space=pltpu.SMEM),
            pl.BlockSpec((1, tq, HEADS_PER_GROUP * hd), lambda bi, gi, qi: (bi, qi, gi)),
            pl.BlockSpec((1, n_seg, hd), lambda bi, gi, qi: (bi * NSA_KV_GROUPS + gi, 0, 0)),
            pl.BlockSpec((1, hd, n_seg), lambda bi, gi, qi: (bi * NSA_KV_GROUPS + gi, 0, 0)),
            pl.BlockSpec((1, s, hd), lambda bi, gi, qi: (bi, 0, kv_col(2) + gi)),
            pl.BlockSpec((1, hd, s), lambda bi, gi, qi: (bi * NSA_KV_GROUPS + gi, 0, 0)),
            pl.BlockSpec((1, s, hd), lambda bi, gi, qi: (bi, 0, kv_col(4) + gi)),
            pl.BlockSpec((1, hd, s), lambda bi, gi, qi: (bi * NSA_KV_GROUPS + gi, 0, 0)),
            pl.BlockSpec((1, 1, gw, tq), lambda bi, gi, qi: (bi, gi, 0, qi)),
            pl.BlockSpec((n_sel, n_seg), lambda bi, gi, qi: (0, 0)),
        ],
        out_specs=pl.BlockSpec((1, tq, HEADS_PER_GROUP * hd), lambda bi, gi, qi: (bi, qi, gi)),
        out_shape=jax.ShapeDtypeStruct((b, s, NSA_Q_WIDTH), big.dtype),
        scratch_shapes=[
            pltpu.VMEM((n_sel, tq), jnp.float32),
            pltpu.VMEM((HEADS_PER_GROUP, tk, tq), jnp.float32),
            pltpu.VMEM((HEADS_PER_GROUP, hd, tq), jnp.float32),
            pltpu.VMEM((2 * HEADS_PER_GROUP, 8, tq), jnp.float32),
            pltpu.VMEM((HEADS_PER_GROUP, hd, tq), jnp.float32),
        ],
        compiler_params=_params(("parallel", "parallel", "arbitrary")),
    )(slopes, big3, kc, vc_t, big3, vsel_t, big3, vwin_t, gate_t, ovl)


_TN = (((0,), (0,)), ((), ()))
_HI = lax.Precision.HIGHEST


def _gla_kernel(q_ref, k_ref, v_ref, og_ref, sm_ref, wa_ref, ba_ref, gn_ref, selw_ref, o_ref,
                state_ref, *, n_chunks):
    f32 = jnp.float32
    mxu_dtype = v_ref.dtype
    c_len, sub = GLA_CHUNK, GLA_SUB
    ns = c_len // sub
    dk = GLA_KEY_DIM

    @pl.when(pl.program_id(2) == 0)
    def _():
        state_ref[...] = jnp.zeros_like(state_ref)

    row = lax.broadcasted_iota(jnp.int32, (c_len, 1), 0)
    col = lax.broadcasted_iota(jnp.int32, (1, c_len), 1)
    tril = (col <= row).astype(f32)
    sub_row = row // sub
    sub_col = col // sub
    t_loc = lax.broadcasted_iota(jnp.int32, (1, sub, 1), 1)
    neg_inf = -jnp.inf

    for c in range(n_chunks):
        rows = slice(c * c_len, (c + 1) * c_len)
        q = q_ref[0, rows, :].astype(f32)
        k = k_ref[0, rows, :].astype(f32)
        v = v_ref[0, rows, :]
        logits = jnp.dot(sm_ref[0, rows, :], wa_ref[0], precision=_HI,
                         preferred_element_type=f32) + ba_ref[0]
        glog = (jnp.minimum(logits, 0.0) - jnp.log(1.0 + jnp.exp(-jnp.abs(logits)))) * (1.0 / GATE_TAU)
        bcum = jnp.dot(tril, glog, precision=_HI, preferred_element_type=f32)
        b_last = bcum[c_len - 1:c_len, :]
        st = state_ref[...]

        q_in = (q * jnp.exp(bcum)).astype(mxu_dtype)
        o = lax.dot_general(q_in, st.astype(mxu_dtype), _NT, preferred_element_type=f32)

        r = [bcum[(j + 1) * sub - 1:(j + 1) * sub, :] for j in range(ns)]
        r_rows = jnp.concatenate([jnp.broadcast_to(rj, (sub, dk)) for rj in r], axis=0)
        kf = (k * jnp.exp(r_rows - bcum)).astype(mxu_dtype)
        a = jnp.zeros((c_len, c_len), f32)
        for j in range(ns - 1):
            arg = jnp.where(row >= (j + 1) * sub, bcum - r[j], neg_inf)
            qf = (q * jnp.exp(arg)).astype(mxu_dtype)
            aj = lax.dot_general(qf, kf, _NT, preferred_element_type=f32)
            a = a + jnp.where(sub_col == j, aj, 0.0)

        q3 = q.reshape(ns, sub, dk)
        k3 = k.reshape(ns, sub, dk)
        b3 = bcum.reshape(ns, sub, dk)
        pieces = []
        for s_ in range(sub):
            arg = jnp.where(t_loc >= s_, b3 - b3[:, s_:s_ + 1, :], neg_inf)
            e = q3 * k3[:, s_:s_ + 1, :] * jnp.exp(arg)
            pieces.append(e.reshape(c_len, dk).astype(mxu_dtype))
        d_wide = jnp.dot(jnp.concatenate(pieces, axis=1), selw_ref[...], preferred_element_type=f32)
        a = a + jnp.where(sub_row == sub_col, d_wide, 0.0)
        o = o + jnp.dot(a.astype(mxu_dtype), v, preferred_element_type=f32)

        kd = (k * jnp.exp(b_last - bcum)).astype(mxu_dtype)
        state_ref[...] = st * jnp.exp(b_last) + lax.dot_general(v, kd, _TN, preferred_element_type=f32)

        rms = lax.rsqrt(jnp.mean(o * o, axis=-1, keepdims=True) + EPS)
        gate = og_ref[0, rows, :].astype(f32)
        y = o * rms * gn_ref[...] * (gate * jax.nn.sigmoid(gate))
        o_ref[0, rows, :] = y.astype(o_ref.dtype)


def _gla_stage(big, small, w_alpha2, b_alpha, g_norm, b, s, tc=256):
    h, dk, dv = GLA_HEADS, GLA_KEY_DIM, GLA_VAL_DIM
    tc = min(tc, s)
    big3 = big.reshape(b, s, BIG_WIDTH)
    small3 = small.reshape(b, s, SMALL_WIDTH)
    wa = jnp.zeros((h, SMALL_WIDTH, dk), jnp.float32).at[:, SMALL_ALPHA:SMALL_ALPHA + GATE_RANK, :].set(
        w_alpha2.astype(jnp.float32).reshape(GATE_RANK, h, dk).transpose(1, 0, 2))
    ba = b_alpha.astype(jnp.float32).reshape(h, 1, dk)
    gn = g_norm.astype(jnp.float32).reshape(1, dv)
    sel = (np.arange(GLA_SUB * dk)[:, None] // dk == np.arange(GLA_CHUNK)[None, :] % GLA_SUB)
    selw = jnp.asarray(sel.astype(np.float32), big.dtype)
    kernel = functools.partial(_gla_kernel, n_chunks=tc // GLA_CHUNK)
    return pl.pallas_call(
        kernel,
        grid=(b, h, s // tc),
        in_specs=[
            pl.BlockSpec((1, tc, dk), lambda bi, hi, ci: (bi, ci, COL_GLA_Q // dk + hi)),
            pl.BlockSpec((1, tc, dk), lambda bi, hi, ci: (bi, ci, COL_GLA_K // dk + hi)),
            pl.BlockSpec((1, tc, dv), lambda bi, hi, ci: (bi, ci, COL_GLA_V // dv + hi)),
            pl.BlockSpec((1, tc, dv), lambda bi, hi, ci: (bi, ci, COL_GLA_GATE // dv + hi)),
            pl.BlockSpec((1, tc, SMALL_WIDTH), lambda bi, hi, ci: (bi, ci, 0)),
            pl.BlockSpec((1, SMALL_WIDTH, dk), lambda bi, hi, ci: (hi, 0, 0)),
            pl.BlockSpec((1, 1, dk), lambda bi, hi, ci: (hi, 0, 0)),
            pl.BlockSpec((1, dv), lambda bi, hi, ci: (0, 0)),
            pl.BlockSpec((GLA_SUB * dk, GLA_CHUNK), lambda bi, hi, ci: (0, 0)),
        ],
        out_specs=pl.BlockSpec((1, tc, dv), lambda bi, hi, ci: (bi, ci, hi)),
        out_shape=jax.ShapeDtypeStruct((b, s, GLA_V_WIDTH), big.dtype),
        scratch_shapes=[pltpu.VMEM((dv, dk), jnp.float32)],
        compiler_params=_params(("parallel", "parallel", "arbitrary")),
    )(big3, big3, big3, big3, small3, wa, ba, gn, selw)


ROUTE_E0 = N_EXPERT_GROUPS
ROUTE_ID = ROUTE_E0 + N_EXPERTS
ROUTE_W = ROUTE_ID + 2
MOE_TM = 256
GATHER_UNROLL = 8


def _out_proj_kernel(x_ref, nsa_ref, gla_ref, wt_ref, wb_ref, g_ref, wr_ref, br_ref,
                     x1_ref, h_ref, comb_ref):
    f32 = jnp.float32
    acc = jnp.dot(nsa_ref[...], wt_ref[...], preferred_element_type=f32)
    acc = acc + jnp.dot(gla_ref[...], wb_ref[...], preferred_element_type=f32)
    x1 = x_ref[...] + acc
    x1_ref[...] = x1
    hf = x1 * lax.rsqrt(jnp.mean(x1 * x1, axis=-1, keepdims=True) + EPS) * g_ref[...]
    h_ref[...] = hf.astype(h_ref.dtype)

    h_hi, h_mid, _ = _split3(hf)
    logits = (jnp.dot(h_hi, wr_ref[0], preferred_element_type=f32)
              + jnp.dot(h_hi, wr_ref[1], preferred_element_type=f32)
              + jnp.dot(h_mid, wr_ref[0], preferred_element_type=f32)) + br_ref[...]

    lane = lax.broadcasted_iota(jnp.int32, (1, LANES), 1).astype(f32)
    big_lane = float(LANES)
    neg_inf = -jnp.inf
    first_argmax = lambda vals, vmax: jnp.min(jnp.where(vals == vmax, lane, big_lane), axis=-1, keepdims=True)
    gl = jnp.where(lane < N_EXPERT_GROUPS, logits, neg_inf)
    gmax = jnp.max(gl, axis=-1, keepdims=True)
    gsel = first_argmax(gl, gmax)
    gw = 1.0 / jnp.sum(jnp.exp(gl - gmax), axis=-1, keepdims=True)
    lo = ROUTE_E0 + EXPERTS_PER_GROUP * gsel
    el = jnp.where(lane >= lo, jnp.where(lane < lo + EXPERTS_PER_GROUP, logits, neg_inf), neg_inf)
    m1 = jnp.max(el, axis=-1, keepdims=True)
    i1 = first_argmax(el, m1)
    el2 = jnp.where(lane == i1, neg_inf, el)
    m2 = jnp.max(el2, axis=-1, keepdims=True)
    i2 = first_argmax(el2, m2)
    e2 = jnp.exp(m2 - m1)
    w1 = gw / (1.0 + e2)
    w2 = w1 * e2
    onehot = jnp.where(lane == i1, 1.0, 0.0) + jnp.where(lane == i2, 1.0, 0.0)
    meta = (jnp.where(lane == ROUTE_ID, i1 - ROUTE_E0, 0.0) + jnp.where(lane == ROUTE_ID + 1, i2 - ROUTE_E0, 0.0)
            + jnp.where(lane == ROUTE_W, w1, 0.0) + jnp.where(lane == ROUTE_W + 1, w2, 0.0))
    comb_ref[...] = onehot + meta


def _out_proj_stage(x2, nsa, gla, w_out, g_ffn, w_rg, b_rg, w_re, b_re, tm=512):
    t, d = x2.shape
    tm = min(tm, t)
    half = nsa.shape[1]
    w_top = w_out[:half].astype(nsa.dtype)
    w_bot = w_out[half:].astype(nsa.dtype)
    wr = jnp.concatenate([w_rg, w_re.reshape(d, N_EXPERTS),
                          jnp.zeros((d, LANES - ROUTE_E0 - N_EXPERTS), jnp.float32)], axis=1)
    wr_hi, wr_mid, _ = _split3(wr)
    wr2 = jnp.stack([wr_hi, wr_mid])
    br = jnp.concatenate([b_rg, b_re.reshape(N_EXPERTS),
                          jnp.zeros((LANES - ROUTE_E0 - N_EXPERTS,), jnp.float32)])[None]
    row = lambda m: (m, 0)
    fixed = lambda m: (0, 0)
    return pl.pallas_call(
        _out_proj_kernel,
        grid=(t // tm,),
        in_specs=[
            pl.BlockSpec((tm, d), row),
            pl.BlockSpec((tm, half), row),
            pl.BlockSpec((tm, half), row),
            pl.BlockSpec((half, d), fixed),
            pl.BlockSpec((half, d), fixed),
            pl.BlockSpec((1, d), fixed),
            pl.BlockSpec((2, d, LANES), lambda m: (0, 0, 0)),
            pl.BlockSpec((1, LANES), fixed),
        ],
        out_specs=[pl.BlockSpec((tm, d), row), pl.BlockSpec((tm, d), row), pl.BlockSpec((tm, LANES), row)],
        out_shape=[
            jax.ShapeDtypeStruct((t, d), jnp.float32),
            jax.ShapeDtypeStruct((t, d), jnp.float32),
            jax.ShapeDtypeStruct((t, LANES), jnp.float32),
        ],
        compiler_params=_params(("parallel",)),
    )(x2, nsa, gla, w_top, w_bot, g_ffn.astype(jnp.float32)[None], wr2, br)


def _rank_kernel(route_ref, rank_ref, count_ref, carry_ref):
    f32 = jnp.float32
    tm = route_ref.shape[0]

    @pl.when(pl.program_id(0) == 0)
    def _():
        carry_ref[...] = jnp.zeros_like(carry_ref)

    route = route_ref[...]
    lane = lax.broadcasted_iota(jnp.int32, (1, LANES), 1)
    lane_f = lane.astype(f32)
    onehot = jnp.where((lane >= ROUTE_E0) & (lane < ROUTE_ID), route, 0.0)
    row = lax.broadcasted_iota(jnp.int32, (tm, tm), 0)
    col = lax.broadcasted_iota(jnp.int32, (tm, tm), 1)
    strict_lower = (col < row).astype(jnp.bfloat16)
    before = jnp.dot(strict_lower, onehot.astype(jnp.bfloat16), preferred_element_type=f32) + carry_ref[0:1, :]
    ranks = []
    for j in range(2):
        lane_j = route[:, ROUTE_ID + j:ROUTE_ID + j + 1] + float(ROUTE_E0)
        ranks.append(jnp.sum(jnp.where(lane_f == lane_j, before, 0.0), axis=-1, keepdims=True))
    rank_ref[...] = jnp.where(lane == 0, ranks[0], jnp.where(lane == 1, ranks[1], 0.0))
    total = carry_ref[0:1, :] + jnp.sum(onehot, axis=0, keepdims=True)
    carry_ref[...] = jnp.broadcast_to(total, carry_ref.shape)
    count_ref[...] = jnp.broadcast_to(total, count_ref.shape)


def _rank_stage(route, tm=512):
    t = route.shape[0]
    tm = min(tm, t)
    return pl.pallas_call(
        _rank_kernel,
        grid=(t // tm,),
        in_specs=[pl.BlockSpec((tm, LANES), lambda m: (m, 0))],
        out_specs=[pl.BlockSpec((tm, LANES), lambda m: (m, 0)), pl.BlockSpec((8, LANES), lambda m: (0, 0))],
        out_shape=[jax.ShapeDtypeStruct((t, LANES), jnp.float32), jax.ShapeDtypeStruct((8, LANES), jnp.float32)],
        scratch_shapes=[pltpu.VMEM((8, LANES), jnp.float32)],
        compiler_params=_params(("arbitrary",)),
    )(route)


def _gather_rows(idx_ref, n_rows, src_hbm, dst, sem):
    def body(r, carry):
        tok = idx_ref[0, 0, r]
        pltpu.make_async_copy(src_hbm.at[pl.ds(tok, 1), :], dst.at[pl.ds(r, 1), :], sem).start()
        return carry
    lax.fori_loop(0, n_rows, body, 0, unroll=GATHER_UNROLL)


def _wait_rows(n_rows, src_hbm, dst, sem):
    pltpu.make_async_copy(src_hbm.at[pl.ds(0, n_rows), :], dst, sem).wait()


def _moe_kernel(te_ref, na_ref, idx_ref, idx_next_ref, h_hbm, wg_ref, wu_ref, wd_ref, y_ref, buf, sem):
    f32 = jnp.float32
    i = pl.program_id(0)
    n_active = na_ref[0]
    tm = buf.shape[1]

    @pl.when((i == 0) & (n_active > 0))
    def _():
        _gather_rows(idx_ref, tm, h_hbm, buf.at[0], sem.at[0])

    @pl.when(i + 1 < n_active)
    def _():
        nxt = (i + 1) % 2
        _gather_rows(idx_next_ref, tm, h_hbm, buf.at[nxt], sem.at[nxt])

    @pl.when(i < n_active)
    def _():
        slot = i % 2
        _wait_rows(tm, h_hbm, buf.at[slot], sem.at[slot])
        xb = buf[slot].astype(jnp.bfloat16)
        gate = jnp.dot(xb, wg_ref[0], preferred_element_type=f32)
        up = jnp.dot(xb, wu_ref[0], preferred_element_type=f32)
        hid = (gate * jax.nn.sigmoid(gate) * up).astype(jnp.bfloat16)
        y_ref[...] = jnp.dot(hid, wd_ref[0], preferred_element_type=f32)

    @pl.when(i >= n_active)
    def _():
        y_ref[...] = jnp.zeros_like(y_ref)


def _moe_stage(h2, tok_of_pos, tile_expert, n_active, w_gate, w_up, w_down):
    t, d = h2.shape
    n_tiles, _, tm = tok_of_pos.shape
    f = w_gate.shape[-1]
    wg = w_gate.reshape(N_EXPERTS, d, f).astype(jnp.bfloat16)
    wu = w_up.reshape(N_EXPERTS, d, f).astype(jnp.bfloat16)
    wd = w_down.reshape(N_EXPERTS, f, d).astype(jnp.bfloat16)
    grid_spec = pltpu.PrefetchScalarGridSpec(
        num_scalar_prefetch=2,
        grid=(n_tiles,),
        in_specs=[
            pl.BlockSpec((1, 1, tm), lambda i, te, na: (i, 0, 0), memory_space=pltpu.SMEM),
            pl.BlockSpec((1, 1, tm), lambda i, te, na: (jnp.minimum(i + 1, n_tiles - 1), 0, 0),
                         memory_space=pltpu.SMEM),
            pl.BlockSpec(memory_space=pl.ANY),
            pl.BlockSpec((1, d, f), lambda i, te, na: (te[i], 0, 0)),
            pl.BlockSpec((1, d, f), lambda i, te, na: (te[i], 0, 0)),
            pl.BlockSpec((1, f, d), lambda i, te, na: (te[i], 0, 0)),
        ],
        out_specs=pl.BlockSpec((tm, d), lambda i, te, na: (i, 0)),
        scratch_shapes=[pltpu.VMEM((2, tm, d), jnp.float32), pltpu.SemaphoreType.DMA((2,))],
    )
    return pl.pallas_call(
        _moe_kernel,
        grid_spec=grid_spec,
        out_shape=jax.ShapeDtypeStruct((n_tiles * tm, d), jnp.float32),
        compiler_params=_params(("arbitrary",)),
    )(tile_expert, n_active, tok_of_pos, tok_of_pos, h2, wg, wu, wd)


def _combine_kernel(p1_ref, p2_ref, p1n_ref, p2n_ref, route_ref, x1_ref, gf_ref, y_hbm, o_ref, buf, sem,
                    *, final_norm):
    i = pl.program_id(0)
    n = pl.num_programs(0)
    tm = buf.shape[2]

    def gather(a_ref, b_ref, slot):
        _gather_rows(a_ref, tm, y_hbm, buf.at[slot, 0], sem.at[slot, 0])
        _gather_rows(b_ref, tm, y_hbm, buf.at[slot, 1], sem.at[slot, 1])

    @pl.when(i == 0)
    def _():
        gather(p1_ref, p2_ref, 0)

    @pl.when(i + 1 < n)
    def _():
        gather(p1n_ref, p2n_ref, (i + 1) % 2)

    slot = i % 2
    _wait_rows(tm, y_hbm, buf.at[slot, 0], sem.at[slot, 0])
    _wait_rows(tm, y_hbm, buf.at[slot, 1], sem.at[slot, 1])
    route = route_ref[...]
    w1 = route[:, ROUTE_W:ROUTE_W + 1]
    w2 = route[:, ROUTE_W + 1:ROUTE_W + 2]
    y = x1_ref[...] + (w1 * buf[slot, 0] + w2 * buf[slot, 1])
    if final_norm:
        y = y * lax.rsqrt(jnp.mean(y * y, axis=-1, keepdims=True) + EPS) * gf_ref[...]
    o_ref[...] = y


def _combine_stage(y_sorted, pos1, pos2, route, x1, g_final, final_norm, tm=256):
    t, d = x1.shape
    tm = min(tm, t)
    n = t // tm
    p1 = pos1.reshape(n, 1, tm)
    p2 = pos2.reshape(n, 1, tm)
    cur = lambda i: (i, 0, 0)
    nxt = lambda i: (jnp.minimum(i + 1, n - 1), 0, 0)
    smem = lambda index_map: pl.BlockSpec((1, 1, tm), index_map, memory_space=pltpu.SMEM)
    return pl.pallas_call(
        functools.partial(_combine_kernel, final_norm=final_norm),
        grid=(n,),
        in_specs=[
            smem(cur), smem(cur), smem(nxt), smem(nxt),
            pl.BlockSpec((tm, LANES), lambda i: (i, 0)),
            pl.BlockSpec((tm, d), lambda i: (i, 0)),
            pl.BlockSpec((1, d), lambda i: (0, 0)),
            pl.BlockSpec(memory_space=pl.ANY),
        ],
        out_specs=pl.BlockSpec((tm, d), lambda i: (i, 0)),
        out_shape=jax.ShapeDtypeStruct((t, d), jnp.float32),
        scratch_shapes=[pltpu.VMEM((2, 2, tm, d), jnp.float32), pltpu.SemaphoreType.DMA((2, 2))],
        compiler_params=_params(("arbitrary",)),
    )(p1, p2, p1, p2, route, x1, g_final.astype(jnp.float32)[None], y_sorted)


def _routing_tables(route, rank, count, tm):
    t = route.shape[0]
    e1 = route[:, ROUTE_ID].astype(jnp.int32)
    e2 = route[:, ROUTE_ID + 1].astype(jnp.int32)
    counts = count[0, ROUTE_E0:ROUTE_E0 + N_EXPERTS].astype(jnp.int32)
    padded = ((counts + tm - 1) // tm) * tm
    ends = jnp.cumsum(padded)
    offs = ends - padded
    pos1 = offs[e1] + rank[:, 0].astype(jnp.int32)
    pos2 = offs[e2] + rank[:, 1].astype(jnp.int32)
    n_tiles = (2 * t) // tm + N_EXPERTS
    tok = jnp.arange(t, dtype=jnp.int32)
    tok_of_pos = jnp.zeros((n_tiles * tm,), jnp.int32).at[jnp.concatenate([pos1, pos2])].set(
        jnp.concatenate([tok, tok]), unique_indices=True)
    tile_start = jnp.arange(n_tiles, dtype=jnp.int32) * tm
    tile_expert = jnp.minimum(jnp.sum((ends[None, :] <= tile_start[:, None]).astype(jnp.int32), axis=1),
                              N_EXPERTS - 1)
    n_active = (ends[-1] // tm).astype(jnp.int32).reshape(1)
    return pos1, pos2, tok_of_pos.reshape(n_tiles, 1, tm), tile_expert, n_active


def kernel(x, g_mix_norm, w_in, b_nsa_gate, cmp_pos_k, w_cmp_k1, w_cmp_k2, cmp_pos_v, w_cmp_v1, w_cmp_v2,
           w_alpha2, b_alpha, g_gla_norm, w_out, g_ffn_norm, w_router_group, b_router_group,
           w_router_expert, b_router_expert, w_expert_gate, w_expert_up, w_expert_down, g_final_norm):
    b, s, d = x.shape
    depth = w_in.shape[0]
    x2 = x.reshape(b * s, d)
    for l in range(depth):
        last = l == depth - 1
        w_big, w_small, col_scale, small_bias = _prep_in_proj_weights(w_in[l], b_nsa_gate[l])
        big, small = _in_proj(x2, g_mix_norm[l].astype(jnp.float32)[None], w_big, w_small, col_scale, small_bias)
        cmp = _compress_stage(big, b, s, cmp_pos_k[l], w_cmp_k1[l], w_cmp_k2[l],
                              cmp_pos_v[l], w_cmp_v1[l], w_cmp_v2[l])
        nsa = _nsa_stage(big, small, cmp, b, s).reshape(b * s, NSA_Q_WIDTH)
        gla = _gla_stage(big, small, w_alpha2[l], b_alpha[l], g_gla_norm[l], b, s).reshape(b * s, GLA_V_WIDTH)
        x1, h2, route = _out_proj_stage(x2, nsa, gla, w_out[l], g_ffn_norm[l], w_router_group[l],
                                        b_router_group[l], w_router_expert[l], b_router_expert[l])
        rank, count = _rank_stage(route)
        pos1, pos2, tok_of_pos, tile_expert, n_active = _routing_tables(route, rank, count, MOE_TM)
        y_sorted = _moe_stage(h2, tok_of_pos, tile_expert, n_active,
                              w_expert_gate[l], w_expert_up[l], w_expert_down[l])
        x2 = _combine_stage(y_sorted, pos1, pos2, route, x1, g_final_norm, last)
    return x2.reshape(b, s, d)
```

```python
import functools

import numpy as np
import jax
import jax.numpy as jnp
from jax import lax
from jax.experimental import pallas as pl
from jax.experimental.pallas import tpu as pltpu

NSA_HEADS = 8
NSA_KV_GROUPS = 2
HEADS_PER_GROUP = NSA_HEADS // NSA_KV_GROUPS
HEAD_DIM = 128
CMP_BLOCK = 32
CMP_STRIDE = 16
SEL_BLOCK = 64
SEL_TOPK = 8
N_LOCAL_BLOCKS = 2
WINDOW = 512
GLA_HEADS = 4
GLA_KEY_DIM = 128
GLA_VAL_DIM = 256
GATE_RANK = 16
GATE_TAU = 16.0
GLA_CHUNK = 64
GLA_SUB = 16
N_EXPERT_GROUPS = 4
EXPERTS_PER_GROUP = 8
N_EXPERTS = N_EXPERT_GROUPS * EXPERTS_PER_GROUP
EPS = 1e-6
MASK_VALUE = -1e30

NSA_Q_WIDTH = NSA_HEADS * HEAD_DIM
NSA_KV_WIDTH = NSA_KV_GROUPS * HEAD_DIM
GLA_QK_WIDTH = GLA_HEADS * GLA_KEY_DIM
GLA_V_WIDTH = GLA_HEADS * GLA_VAL_DIM

LANES = 128
VMEM_LIMIT_BYTES = 56 * 1024 * 1024

COL_NSA_Q = 0
COL_KV = COL_NSA_Q + NSA_Q_WIDTH
COL_GLA_Q = COL_KV + 6 * NSA_KV_WIDTH
COL_GLA_K = COL_GLA_Q + GLA_QK_WIDTH
COL_GLA_V = COL_GLA_K + GLA_QK_WIDTH
COL_GLA_GATE = COL_GLA_V + GLA_V_WIDTH
BIG_WIDTH = COL_GLA_GATE + GLA_V_WIDTH
SMALL_GATE = 0
SMALL_ALPHA = 3 * NSA_HEADS
SMALL_WIDTH = LANES


def _params(semantics):
    return pltpu.CompilerParams(dimension_semantics=semantics, vmem_limit_bytes=VMEM_LIMIT_BYTES)


def _split3(a):
    hi = a.astype(jnp.bfloat16)
    r1 = a - hi.astype(jnp.float32)
    mid = r1.astype(jnp.bfloat16)
    lo = (r1 - mid.astype(jnp.float32)).astype(jnp.bfloat16)
    return hi, mid, lo


def _in_proj_kernel(x_ref, g_ref, wb_ref, ws_ref, scale_ref, sbias_ref, big_ref, small_ref, h_ref):
    n = pl.program_id(1)

    @pl.when(n == 0)
    def _():
        xf = x_ref[...]
        y = xf * lax.rsqrt(jnp.mean(xf * xf, axis=-1, keepdims=True) + EPS)
        h = (y * g_ref[...]).astype(jnp.bfloat16)
        h_ref[...] = h
        small_ref[...] = jnp.dot(h, ws_ref[...], preferred_element_type=jnp.float32) + sbias_ref[...]

    acc = jnp.dot(h_ref[...], wb_ref[...], preferred_element_type=jnp.float32)
    big_ref[...] = (acc * scale_ref[...]).astype(big_ref.dtype)


def _in_proj(x2, gain, w_big, w_small, col_scale, small_bias, tm=1024, tn=1408):
    t, d = x2.shape
    tm = min(tm, t)
    grid = (t // tm, BIG_WIDTH // tn)
    return pl.pallas_call(
        _in_proj_kernel,
        grid=grid,
        in_specs=[
            pl.BlockSpec((tm, d), lambda m, n: (m, 0)),
            pl.BlockSpec((1, d), lambda m, n: (0, 0)),
            pl.BlockSpec((d, tn), lambda m, n: (0, n)),
            pl.BlockSpec((d, SMALL_WIDTH), lambda m, n: (0, 0)),
            pl.BlockSpec((1, tn), lambda m, n: (0, n)),
            pl.BlockSpec((1, SMALL_WIDTH), lambda m, n: (0, 0)),
        ],
        out_specs=[
            pl.BlockSpec((tm, tn), lambda m, n: (m, n)),
            pl.BlockSpec((tm, SMALL_WIDTH), lambda m, n: (m, 0)),
        ],
        out_shape=[
            jax.ShapeDtypeStruct((t, BIG_WIDTH), jnp.bfloat16),
            jax.ShapeDtypeStruct((t, SMALL_WIDTH), jnp.float32),
        ],
        scratch_shapes=[pltpu.VMEM((tm, d), jnp.bfloat16)],
        compiler_params=_params(("parallel", "arbitrary")),
    )(x2, gain, w_big, w_small, col_scale, small_bias)


def _prep_in_proj_weights(w_in, b_nsa_gate):
    sizes = (NSA_Q_WIDTH,) + (NSA_KV_WIDTH,) * 6 + (3 * NSA_HEADS, GLA_QK_WIDTH, GLA_QK_WIDTH,
                                                    GLA_V_WIDTH, GATE_RANK, GLA_V_WIDTH)
    offs = np.concatenate([[0], np.cumsum(sizes)])
    part = lambda i: w_in[:, offs[i]:offs[i + 1]]
    w_big = jnp.concatenate([part(i) for i in (0, 1, 2, 3, 4, 5, 6, 8, 9, 10, 12)], axis=1)
    pad = jnp.zeros((w_in.shape[0], SMALL_WIDTH - 3 * NSA_HEADS - GATE_RANK), w_in.dtype)
    w_small = jnp.concatenate([part(7), part(11), pad], axis=1)
    scale = np.ones((1, BIG_WIDTH), np.float32)
    scale[0, COL_NSA_Q:COL_NSA_Q + NSA_Q_WIDTH] = HEAD_DIM ** -0.5
    scale[0, COL_GLA_Q:COL_GLA_Q + GLA_QK_WIDTH] = GLA_KEY_DIM ** -0.5
    small_bias = jnp.concatenate([b_nsa_gate.astype(jnp.float32),
                                  jnp.zeros((SMALL_WIDTH - 3 * NSA_HEADS,), jnp.float32)])[None]
    return w_big.astype(jnp.bfloat16), w_small.astype(jnp.bfloat16), jnp.asarray(scale), small_bias


def _gelu_tanh(a):
    return 0.5 * a * (1.0 + jnp.tanh(np.sqrt(2.0 / np.pi) * (a + 0.044715 * (a * a * a))))


def _compress_kernel(z_ref, pos_ref, w1_ref, w2_ref, o_ref):
    z = z_ref[0].astype(jnp.float32)
    pos = pos_ref[0]
    z_lo = (z + pos[0:1]).astype(jnp.bfloat16)
    z_hi = (z + pos[1:2]).astype(jnp.bfloat16)
    u = jnp.dot(z_lo, w1_ref[0, 0], preferred_element_type=jnp.float32)
    v = jnp.dot(z_hi, w1_ref[0, 1], preferred_element_type=jnp.float32)
    a = u + pltpu.roll(v, v.shape[0] - 1, axis=0)
    hid = _gelu_tanh(a).astype(jnp.bfloat16)
    o_ref[0] = jnp.dot(hid, w2_ref[0], preferred_element_type=jnp.float32).astype(o_ref.dtype)


def _compress(z, pos, w1, w2, tr):
    _, rows, width = z.shape
    hid = w1.shape[-1]
    return pl.pallas_call(
        _compress_kernel,
        grid=(2, rows // tr),
        in_specs=[
            pl.BlockSpec((1, tr, width), lambda c, r: (c, r, 0)),
            pl.BlockSpec((1, 2, width), lambda c, r: (c, 0, 0)),
            pl.BlockSpec((1, 2, width, hid), lambda c, r: (c, 0, 0, 0)),
            pl.BlockSpec((1, hid, HEAD_DIM), lambda c, r: (c, 0, 0)),
        ],
        out_specs=pl.BlockSpec((1, tr, HEAD_DIM), lambda c, r: (c, r, 0)),
        out_shape=jax.ShapeDtypeStruct((2, rows, HEAD_DIM), jnp.bfloat16),
        compiler_params=_params(("parallel", "parallel")),
    )(z, pos, w1, w2)


def _compress_stage(big, b, s, cmp_pos_k, w_cmp_k1, w_cmp_k2, cmp_pos_v, w_cmp_v1, w_cmp_v2):
    g, dk = NSA_KV_GROUPS, HEAD_DIM
    n_seg = s // CMP_STRIDE
    kv = big[:, COL_KV:COL_KV + 2 * NSA_KV_WIDTH].reshape(b, n_seg, CMP_STRIDE, 2, g, dk)
    z = kv.transpose(3, 0, 4, 1, 2, 5).reshape(2, b * g * n_seg, CMP_STRIDE * dk)
    pos = jnp.stack([cmp_pos_k, cmp_pos_v]).reshape(2, 2, CMP_STRIDE * dk)
    w1 = jnp.stack([w_cmp_k1, w_cmp_v1]).reshape(2, 2, CMP_STRIDE * dk, -1).astype(jnp.bfloat16)
    w2 = jnp.stack([w_cmp_k2, w_cmp_v2]).astype(jnp.bfloat16)
    pairs = b * g
    per_tile = max(p for p in (1, 2, 4, 8) if pairs % p == 0 and p * n_seg <= 1024)
    out = _compress(z, pos, w1, w2, per_tile * n_seg)
    return out.reshape(2, b * g, n_seg, dk)


_NT = (((1,), (1,)), ((), ()))


def _nsa_kernel(slopes_ref, q_ref, kc_ref, vct_ref, ksel_ref, vselt_ref, kwin_ref, vwint_ref,
                gate_ref, ovl_ref, o_ref, selb_ref, alibi_ref, acc_ref, ml_ref, out_ref_t, *, tq, tk):
    g = pl.program_id(1)
    qi = pl.program_id(2)
    q0 = qi * tq
    n_seg = kc_ref.shape[1]
    n_sel = ovl_ref.shape[0]
    mxu_dtype = ksel_ref.dtype
    f32 = jnp.float32
    hd = HEAD_DIM

    t_row = q0 + lax.broadcasted_iota(jnp.int32, (1, tq), 1)

    cmp_end = lax.broadcasted_iota(jnp.int32, (n_seg, 1), 0) * CMP_STRIDE + (CMP_BLOCK - 1)
    valid_c = cmp_end <= t_row
    dist_c = (t_row - cmp_end).astype(f32)
    any_c = (t_row >= CMP_BLOCK - 1).astype(f32)
    kc = kc_ref[0]
    vct = vct_ref[0]
    psum = jnp.zeros((n_seg, tq), f32)
    heads = range(HEADS_PER_GROUP)
    slopes = [slopes_ref[g, hh] for hh in heads]
    gates = [jax.nn.sigmoid(gate_ref[0, 0, 3 * hh:3 * hh + 3, :]) for hh in heads]
    for hh in heads:
        qh = q_ref[0, :, hh * hd:(hh + 1) * hd]
        sc = lax.dot_general(kc, qh, _NT, preferred_element_type=f32)
        sc = jnp.where(valid_c, sc - slopes[hh] * dist_c, MASK_VALUE)
        m = jnp.max(sc, axis=0, keepdims=True)
        p = jnp.exp(sc - m)
        p = p * (any_c / jnp.sum(p, axis=0, keepdims=True))
        psum = psum + p
        o_cmp = jnp.dot(vct, p.astype(mxu_dtype), preferred_element_type=f32)
        out_ref_t[hh] = gates[hh][0:1] * o_cmp

    ovl = ovl_ref[...]
    imp = jnp.zeros((n_sel, tq), f32)
    for piece in _split3(psum):
        imp = imp + jnp.dot(ovl, piece, preferred_element_type=f32)
    blk = lax.broadcasted_iota(jnp.int32, (n_sel, 1), 0)
    cur = t_row // SEL_BLOCK
    causal_blk = blk * SEL_BLOCK <= t_row
    forced = (blk == 0) | ((blk <= cur) & (blk > cur - N_LOCAL_BLOCKS))
    imp = jnp.where(forced, -MASK_VALUE, jnp.where(causal_blk, imp, MASK_VALUE))
    cnt = jnp.zeros((n_sel, tq), f32)
    for i in range(n_sel):
        ri = imp[i:i + 1, :]
        beats = jnp.where(blk > i, (ri >= imp).astype(f32), (ri > imp).astype(f32))
        cnt = cnt + beats
    topk = min(SEL_TOPK, n_sel)
    selb_ref[...] = jnp.where(cnt < topk, 0.0, MASK_VALUE)

    d0 = (lax.broadcasted_iota(jnp.int32, (tk, tq), 1)
          - lax.broadcasted_iota(jnp.int32, (tk, tq), 0))
    d0f = d0.astype(f32)
    for hh in heads:
        alibi_ref[hh] = slopes[hh] * d0f
    blocks_per_chunk = tk // SEL_BLOCK

    def reset():
        acc_ref[...] = jnp.zeros_like(acc_ref)
        for hh in heads:
            ml_ref[2 * hh, 0:1, :] = jnp.full((1, tq), MASK_VALUE, f32)
            ml_ref[2 * hh + 1, 0:1, :] = jnp.zeros((1, tq), f32)

    def chunk(k_ref, vt_ref, k0, bias, delta_f):
        kblk = k_ref[0, pl.ds(k0, tk), :]
        vt = vt_ref[0, :, pl.ds(k0, tk)]
        scores = [lax.dot_general(kblk, q_ref[0, :, hh * hd:(hh + 1) * hd], _NT, preferred_element_type=f32)
                  for hh in heads]
        for hh in heads:
            s = scores[hh] - alibi_ref[hh]
            if bias is not None:
                s = s + bias
            cst = -slopes[hh] * delta_f
            m = ml_ref[2 * hh, 0:1, :]
            l = ml_ref[2 * hh + 1, 0:1, :]
            m_new = jnp.maximum(m, jnp.max(s, axis=0, keepdims=True) + cst)
            alpha = jnp.exp(m - m_new)
            p = jnp.exp(s - (m_new - cst))
            ml_ref[2 * hh, 0:1, :] = m_new
            ml_ref[2 * hh + 1, 0:1, :] = alpha * l + jnp.sum(p, axis=0, keepdims=True)
            acc_ref[hh] = alpha * acc_ref[hh] + jnp.dot(vt, p.astype(mxu_dtype), preferred_element_type=f32)

    def finish(branch):
        for hh in heads:
            out_ref_t[hh] += gates[hh][branch:branch + 1] * (acc_ref[hh] / ml_ref[2 * hh + 1, 0:1, :])

    def sel_rows(c):
        rows = [jnp.broadcast_to(selb_ref[pl.ds(c * blocks_per_chunk + j, 1), :], (SEL_BLOCK, tq))
                for j in range(blocks_per_chunk)]
        return jnp.concatenate(rows, axis=0)

    reset()

    def sel_body(c, carry):
        k0 = pl.multiple_of(c * tk, tk)
        chunk(ksel_ref, vselt_ref, k0, sel_rows(c), (q0 - k0).astype(f32))
        return carry

    lax.fori_loop(0, qi, sel_body, 0)
    k_diag = pl.multiple_of(q0, tk)
    chunk(ksel_ref, vselt_ref, k_diag, sel_rows(qi) + jnp.where(d0 >= 0, 0.0, MASK_VALUE), 0.0)
    finish(1)

    reset()
    for delta in range(0, WINDOW + tk, tk):
        lo_ok = delta - (tk - 1) >= 0
        hi_ok = delta + (tq - 1) < WINDOW
        if lo_ok and hi_ok:
            bias = None
        else:
            dist = d0 + delta
            bias = jnp.where(dist >= 0, jnp.where(dist < WINDOW, 0.0, MASK_VALUE), MASK_VALUE)
        if delta == 0:
            chunk(kwin_ref, vwint_ref, k_diag, bias, 0.0)
        else:
            @pl.when(q0 >= delta)
            def _(delta=delta, bias=bias):
                chunk(kwin_ref, vwint_ref, pl.multiple_of(q0 - delta, tk), bias, float(delta))
    finish(2)

    for hh in heads:
        o_ref[0, :, hh * hd:(hh + 1) * hd] = out_ref_t[hh].T.astype(o_ref.dtype)


def _nsa_stage(big, small, cmp, b, s, tq=256):
    g, hd = NSA_KV_GROUPS, HEAD_DIM
    tq = min(tq, s)
    tk = tq
    n_seg = s // CMP_STRIDE
    n_sel = s // SEL_BLOCK
    big3 = big.reshape(b, s, BIG_WIDTH)
    kv_col = lambda which: (COL_KV + which * NSA_KV_WIDTH) // hd
    vt = lambda which: big3[:, :, COL_KV + which * NSA_KV_WIDTH:COL_KV + (which + 1) * NSA_KV_WIDTH] \
        .reshape(b, s, g, hd).transpose(0, 2, 3, 1).reshape(b * g, hd, s)
    vsel_t, vwin_t = vt(3), vt(5)
    kc = cmp[0]
    vc_t = cmp[1].transpose(0, 2, 1)
    gate_t = small[:, SMALL_GATE:SMALL_GATE + 3 * NSA_HEADS].reshape(b, s, g, 3 * HEADS_PER_GROUP) \
        .transpose(0, 2, 3, 1)
    c_start = np.arange(n_seg)[None, :] * CMP_STRIDE
    s_start = np.arange(n_sel)[:, None] * SEL_BLOCK
    n_cmp = (s - CMP_BLOCK) // CMP_STRIDE + 1
    ovl = ((c_start < s_start + SEL_BLOCK) & (c_start + CMP_BLOCK > s_start)
           & (np.arange(n_seg)[None, :] < n_cmp)).astype(np.float32)
    ovl = jnp.asarray(ovl, jnp.bfloat16)
    slopes = jnp.asarray((2.0 ** (-8.0 * np.arange(1, NSA_HEADS + 1) / NSA_HEADS))
                         .reshape(g, HEADS_PER_GROUP), jnp.float32)
    gw = 3 * HEADS_PER_GROUP
    kernel = functools.partial(_nsa_kernel, tq=tq, tk=tk)
    return pl.pallas_call(
        kernel,
        grid=(b, g, s // tq),
        in_specs=[
            pl.BlockSpec(memory_space=pltpu.SMEM),
            pl.BlockSpec((1, tq, HEADS_PER_GROUP * hd), lambda bi, gi, qi: (bi, qi, gi)),
            pl.BlockSpec((1, n_seg, hd), lambda bi, gi, qi: (bi * NSA_KV_GROUPS + gi, 0, 0)),
            pl.BlockSpec((1, hd, n_seg), lambda bi, gi, qi: (bi * NSA_KV_GROUPS + gi, 0, 0)),
            pl.BlockSpec((1, s, hd), lambda bi, gi, qi: (bi, 0, kv_col(2) + gi)),
            pl.BlockSpec((1, hd, s), lambda bi, gi, qi: (bi * NSA_KV_GROUPS + gi, 0, 0)),
            pl.BlockSpec((1, s, hd), lambda bi, gi, qi: (bi, 0, kv_col(4) + gi)),
            pl.BlockSpec((1, hd, s), lambda bi, gi, qi: (bi * NSA_KV_GROUPS + gi, 0, 0)),
            pl.BlockSpec((1, 1, gw, tq), lambda bi, gi, qi: (bi, gi, 0, qi)),
            pl.BlockSpec((n_sel, n_seg), lambda bi, gi, qi: (0, 0)),
        ],
        out_specs=pl.BlockSpec((1, tq, HEADS_PER_GROUP * hd), lambda bi, gi, qi: (bi, qi, gi)),
        out_shape=jax.ShapeDtypeStruct((b, s, NSA_Q_WIDTH), big.dtype),
        scratch_shapes=[
            pltpu.VMEM((n_sel, tq), jnp.float32),
            pltpu.VMEM((HEADS_PER_GROUP, tk, tq), jnp.float32),
            pltpu.VMEM((HEADS_PER_GROUP, hd, tq), jnp.float32),
            pltpu.VMEM((2 * HEADS_PER_GROUP, 8, tq), jnp.float32),
            pltpu.VMEM((HEADS_PER_GROUP, hd, tq), jnp.float32),
        ],
        compiler_params=_params(("parallel", "parallel", "arbitrary")),
    )(slopes, big3, kc, vc_t, big3, vsel_t, big3, vwin_t, gate_t, ovl)


_TN = (((0,), (0,)), ((), ()))
_HI = lax.Precision.HIGHEST


def _gla_kernel(q_ref, k_ref, v_ref, og_ref, sm_ref, wa_ref, ba_ref, gn_ref, selw_ref, o_ref,
                state_ref, *, n_chunks):
    f32 = jnp.float32
    mxu_dtype = v_ref.dtype
    c_len, sub = GLA_CHUNK, GLA_SUB
    ns = c_len // sub
    dk = GLA_KEY_DIM

    @pl.when(pl.program_id(2) == 0)
    def _():
        state_ref[...] = jnp.zeros_like(state_ref)

    row = lax.broadcasted_iota(jnp.int32, (c_len, 1), 0)
    col = lax.broadcasted_iota(jnp.int32, (1, c_len), 1)
    tril = (col <= row).astype(f32)
    sub_row = row // sub
    sub_col = col // sub
    t_loc = lax.broadcasted_iota(jnp.int32, (1, sub, 1), 1)
    neg_inf = -jnp.inf

    for c in range(n_chunks):
        rows = slice(c * c_len, (c + 1) * c_len)
        q = q_ref[0, rows, :].astype(f32)
        k = k_ref[0, rows, :].astype(f32)
        v = v_ref[0, rows, :]
        logits = jnp.dot(sm_ref[0, rows, :], wa_ref[0], precision=_HI,
                         preferred_element_type=f32) + ba_ref[0]
        glog = (jnp.minimum(logits, 0.0) - jnp.log(1.0 + jnp.exp(-jnp.abs(logits)))) * (1.0 / GATE_TAU)
        bcum = jnp.dot(tril, glog, precision=_HI, preferred_element_type=f32)
        b_last = bcum[c_len - 1:c_len, :]
        st = state_ref[...]

        q_in = (q * jnp.exp(bcum)).astype(mxu_dtype)
        o = lax.dot_general(q_in, st.astype(mxu_dtype), _NT, preferred_element_type=f32)

        r = [bcum[(j + 1) * sub - 1:(j + 1) * sub, :] for j in range(ns)]
        r_rows = jnp.concatenate([jnp.broadcast_to(rj, (sub, dk)) for rj in r], axis=0)
        kf = (k * jnp.exp(r_rows - bcum)).astype(mxu_dtype)
        a = jnp.zeros((c_len, c_len), f32)
        for j in range(ns - 1):
            arg = jnp.where(row >= (j + 1) * sub, bcum - r[j], neg_inf)
            qf = (q * jnp.exp(arg)).astype(mxu_dtype)
            aj = lax.dot_general(qf, kf, _NT, preferred_element_type=f32)
            a = a + jnp.where(sub_col == j, aj, 0.0)

        q3 = q.reshape(ns, sub, dk)
        k3 = k.reshape(ns, sub, dk)
        b3 = bcum.reshape(ns, sub, dk)
        pieces = []
        for s_ in range(sub):
            arg = jnp.where(t_loc >= s_, b3 - b3[:, s_:s_ + 1, :], neg_inf)
            e = q3 * k3[:, s_:s_ + 1, :] * jnp.exp(arg)
            pieces.append(e.reshape(c_len, dk).astype(mxu_dtype))
        d_wide = jnp.dot(jnp.concatenate(pieces, axis=1), selw_ref[...], preferred_element_type=f32)
        a = a + jnp.where(sub_row == sub_col, d_wide, 0.0)
        o = o + jnp.dot(a.astype(mxu_dtype), v, preferred_element_type=f32)

        kd = (k * jnp.exp(b_last - bcum)).astype(mxu_dtype)
        state_ref[...] = st * jnp.exp(b_last) + lax.dot_general(v, kd, _TN, preferred_element_type=f32)

        rms = lax.rsqrt(jnp.mean(o * o, axis=-1, keepdims=True) + EPS)
        gate = og_ref[0, rows, :].astype(f32)
        y = o * rms * gn_ref[...] * (gate * jax.nn.sigmoid(gate))
        o_ref[0, rows, :] = y.astype(o_ref.dtype)


def _gla_stage(big, small, w_alpha2, b_alpha, g_norm, b, s, tc=256):
    h, dk, dv = GLA_HEADS, GLA_KEY_DIM, GLA_VAL_DIM
    tc = min(tc, s)
    big3 = big.reshape(b, s, BIG_WIDTH)
    small3 = small.reshape(b, s, SMALL_WIDTH)
    wa = jnp.zeros((h, SMALL_WIDTH, dk), jnp.float32).at[:, SMALL_ALPHA:SMALL_ALPHA + GATE_RANK, :].set(
        w_alpha2.astype(jnp.float32).reshape(GATE_RANK, h, dk).transpose(1, 0, 2))
    ba = b_alpha.astype(jnp.float32).reshape(h, 1, dk)
    gn = g_norm.astype(jnp.float32).reshape(1, dv)
    sel = (np.arange(GLA_SUB * dk)[:, None] // dk == np.arange(GLA_CHUNK)[None, :] % GLA_SUB)
    selw = jnp.asarray(sel.astype(np.float32), big.dtype)
    kernel = functools.partial(_gla_kernel, n_chunks=tc // GLA_CHUNK)
    return pl.pallas_call(
        kernel,
        grid=(b, h, s // tc),
        in_specs=[
            pl.BlockSpec((1, tc, dk), lambda bi, hi, ci: (bi, ci, COL_GLA_Q // dk + hi)),
            pl.BlockSpec((1, tc, dk), lambda bi, hi, ci: (bi, ci, COL_GLA_K // dk + hi)),
            pl.BlockSpec((1, tc, dv), lambda bi, hi, ci: (bi, ci, COL_GLA_V // dv + hi)),
            pl.BlockSpec((1, tc, dv), lambda bi, hi, ci: (bi, ci, COL_GLA_GATE // dv + hi)),
            pl.BlockSpec((1, tc, SMALL_WIDTH), lambda bi, hi, ci: (bi, ci, 0)),
            pl.BlockSpec((1, SMALL_WIDTH, dk), lambda bi, hi, ci: (hi, 0, 0)),
            pl.BlockSpec((1, 1, dk), lambda bi, hi, ci: (hi, 0, 0)),
            pl.BlockSpec((1, dv), lambda bi, hi, ci: (0, 0)),
            pl.BlockSpec((GLA_SUB * dk, GLA_CHUNK), lambda bi, hi, ci: (0, 0)),
        ],
        out_specs=pl.BlockSpec((1, tc, dv), lambda bi, hi, ci: (bi, ci, hi)),
        out_shape=jax.ShapeDtypeStruct((b, s, GLA_V_WIDTH), big.dtype),
        scratch_shapes=[pltpu.VMEM((dv, dk), jnp.float32)],
        compiler_params=_params(("parallel", "parallel", "arbitrary")),
    )(big3, big3, big3, big3, small3, wa, ba, gn, selw)


ROUTE_E0 = N_EXPERT_GROUPS
ROUTE_ID = ROUTE_E0 + N_EXPERTS
ROUTE_W = ROUTE_ID + 2
MOE_TM = 256
GATHER_UNROLL = 8


def _out_proj_kernel(x_ref, nsa_ref, gla_ref, wt_ref, wb_ref, g_ref, wr_ref, br_ref,
                     x1_ref, h_ref, comb_ref):
    f32 = jnp.float32
    acc = jnp.dot(nsa_ref[...], wt_ref[...], preferred_element_type=f32)
    acc = acc + jnp.dot(gla_ref[...], wb_ref[...], preferred_element_type=f32)
    x1 = x_ref[...] + acc
    x1_ref[...] = x1
    hf = x1 * lax.rsqrt(jnp.mean(x1 * x1, axis=-1, keepdims=True) + EPS) * g_ref[...]
    h_ref[...] = hf.astype(h_ref.dtype)

    h_hi, h_mid, _ = _split3(hf)
    logits = (jnp.dot(h_hi, wr_ref[0], preferred_element_type=f32)
              + jnp.dot(h_hi, wr_ref[1], preferred_element_type=f32)
              + jnp.dot(h_mid, wr_ref[0], preferred_element_type=f32)) + br_ref[...]

    lane = lax.broadcasted_iota(jnp.int32, (1, LANES), 1).astype(f32)
    big_lane = float(LANES)
    neg_inf = -jnp.inf
    first_argmax = lambda vals, vmax: jnp.min(jnp.where(vals == vmax, lane, big_lane), axis=-1, keepdims=True)
    gl = jnp.where(lane < N_EXPERT_GROUPS, logits, neg_inf)
    gmax = jnp.max(gl, axis=-1, keepdims=True)
    gsel = first_argmax(gl, gmax)
    gw = 1.0 / jnp.sum(jnp.exp(gl - gmax), axis=-1, keepdims=True)
    lo = ROUTE_E0 + EXPERTS_PER_GROUP * gsel
    el = jnp.where(lane >= lo, jnp.where(lane < lo + EXPERTS_PER_GROUP, logits, neg_inf), neg_inf)
    m1 = jnp.max(el, axis=-1, keepdims=True)
    i1 = first_argmax(el, m1)
    el2 = jnp.where(lane == i1, neg_inf, el)
    m2 = jnp.max(el2, axis=-1, keepdims=True)
    i2 = first_argmax(el2, m2)
    e2 = jnp.exp(m2 - m1)
    w1 = gw / (1.0 + e2)
    w2 = w1 * e2
    onehot = jnp.where(lane == i1, 1.0, 0.0) + jnp.where(lane == i2, 1.0, 0.0)
    meta = (jnp.where(lane == ROUTE_ID, i1 - ROUTE_E0, 0.0) + jnp.where(lane == ROUTE_ID + 1, i2 - ROUTE_E0, 0.0)
            + jnp.where(lane == ROUTE_W, w1, 0.0) + jnp.where(lane == ROUTE_W + 1, w2, 0.0))
    comb_ref[...] = onehot + meta


def _out_proj_stage(x2, nsa, gla, w_out, g_ffn, w_rg, b_rg, w_re, b_re, tm=512):
    t, d = x2.shape
    tm = min(tm, t)
    half = nsa.shape[1]
    w_top = w_out[:half].astype(nsa.dtype)
    w_bot = w_out[half:].astype(nsa.dtype)
    wr = jnp.concatenate([w_rg, w_re.reshape(d, N_EXPERTS),
                          jnp.zeros((d, LANES - ROUTE_E0 - N_EXPERTS), jnp.float32)], axis=1)
    wr_hi, wr_mid, _ = _split3(wr)
    wr2 = jnp.stack([wr_hi, wr_mid])
    br = jnp.concatenate([b_rg, b_re.reshape(N_EXPERTS),
                          jnp.zeros((LANES - ROUTE_E0 - N_EXPERTS,), jnp.float32)])[None]
    row = lambda m: (m, 0)
    fixed = lambda m: (0, 0)
    return pl.pallas_call(
        _out_proj_kernel,
        grid=(t // tm,),
        in_specs=[
            pl.BlockSpec((tm, d), row),
            pl.BlockSpec((tm, half), row),
            pl.BlockSpec((tm, half), row),
            pl.BlockSpec((half, d), fixed),
            pl.BlockSpec((half, d), fixed),
            pl.BlockSpec((1, d), fixed),
            pl.BlockSpec((2, d, LANES), lambda m: (0, 0, 0)),
            pl.BlockSpec((1, LANES), fixed),
        ],
        out_specs=[pl.BlockSpec((tm, d), row), pl.BlockSpec((tm, d), row), pl.BlockSpec((tm, LANES), row)],
        out_shape=[
            jax.ShapeDtypeStruct((t, d), jnp.float32),
            jax.ShapeDtypeStruct((t, d), jnp.float32),
            jax.ShapeDtypeStruct((t, LANES), jnp.float32),
        ],
        compiler_params=_params(("parallel",)),
    )(x2, nsa, gla, w_top, w_bot, g_ffn.astype(jnp.float32)[None], wr2, br)


def _rank_kernel(route_ref, rank_ref, count_ref, carry_ref):
    f32 = jnp.float32
    tm = route_ref.shape[0]

    @pl.when(pl.program_id(0) == 0)
    def _():
        carry_ref[...] = jnp.zeros_like(carry_ref)

    route = route_ref[...]
    lane = lax.broadcasted_iota(jnp.int32, (1, LANES), 1)
    lane_f = lane.astype(f32)
    onehot = jnp.where((lane >= ROUTE_E0) & (lane < ROUTE_ID), route, 0.0)
    row = lax.broadcasted_iota(jnp.int32, (tm, tm), 0)
    col = lax.broadcasted_iota(jnp.int32, (tm, tm), 1)
    strict_lower = (col < row).astype(jnp.bfloat16)
    before = jnp.dot(strict_lower, onehot.astype(jnp.bfloat16), preferred_element_type=f32) + carry_ref[0:1, :]
    ranks = []
    for j in range(2):
        lane_j = route[:, ROUTE_ID + j:ROUTE_ID + j + 1] + float(ROUTE_E0)
        ranks.append(jnp.sum(jnp.where(lane_f == lane_j, before, 0.0), axis=-1, keepdims=True))
    rank_ref[...] = jnp.where(lane == 0, ranks[0], jnp.where(lane == 1, ranks[1], 0.0))
    total = carry_ref[0:1, :] + jnp.sum(onehot, axis=0, keepdims=True)
    carry_ref[...] = jnp.broadcast_to(total, carry_ref.shape)
    count_ref[...] = jnp.broadcast_to(total, count_ref.shape)


def _rank_stage(route, tm=512):
    t = route.shape[0]
    tm = min(tm, t)
    return pl.pallas_call(
        _rank_kernel,
        grid=(t // tm,),
        in_specs=[pl.BlockSpec((tm, LANES), lambda m: (m, 0))],
        out_specs=[pl.BlockSpec((tm, LANES), lambda m: (m, 0)), pl.BlockSpec((8, LANES), lambda m: (0, 0))],
        out_shape=[jax.ShapeDtypeStruct((t, LANES), jnp.float32), jax.ShapeDtypeStruct((8, LANES), jnp.float32)],
        scratch_shapes=[pltpu.VMEM((8, LANES), jnp.float32)],
        compiler_params=_params(("arbitrary",)),
    )(route)


def _gather_rows(idx_ref, n_rows, src_hbm, dst, sem):
    def body(r, carry):
        tok = idx_ref[0, 0, r]
        pltpu.make_async_copy(src_hbm.at[pl.ds(tok, 1), :], dst.at[pl.ds(r, 1), :], sem).start()
        return carry
    lax.fori_loop(0, n_rows, body, 0, unroll=GATHER_UNROLL)


def _wait_rows(n_rows, src_hbm, dst, sem):
    pltpu.make_async_copy(src_hbm.at[pl.ds(0, n_rows), :], dst, sem).wait()


def _moe_kernel(te_ref, na_ref, idx_ref, idx_next_ref, h_hbm, wg_ref, wu_ref, wd_ref, y_ref,
                buf, sem, wg_bf, wu_bf, wd_bf):
    f32 = jnp.float32
    bf16 = jnp.bfloat16
    i = pl.program_id(0)
    n_active = na_ref[0]
    tm = buf.shape[1]
    slot = i % 2

    @pl.when(i == 0)
    def _():
        _gather_rows(idx_ref, tm, h_hbm, buf.at[0], sem.at[0])

    @pl.when(i < n_active)
    def _():
        expert_changed = (i == 0) | (te_ref[i] != te_ref[jnp.maximum(i - 1, 0)])

        @pl.when(expert_changed)
        def _():
            wg_bf[...] = wg_ref[0].astype(bf16)
            wu_bf[...] = wu_ref[0].astype(bf16)
            wd_bf[...] = wd_ref[0].astype(bf16)

        for cur in range(2):
            @pl.when(slot == cur)
            def _(cur=cur):
                nxt = 1 - cur
                _wait_rows(tm, h_hbm, buf.at[cur], sem.at[cur])
                for r in range(tm):
                    tok = idx_next_ref[0, 0, r]
                    pltpu.make_async_copy(h_hbm.at[pl.ds(tok, 1), :], buf.at[nxt, pl.ds(r, 1), :],
                                          sem.at[nxt]).start()
                xb = buf[cur].astype(bf16)
                gate = jnp.dot(xb, wg_bf[...], preferred_element_type=f32)
                up = jnp.dot(xb, wu_bf[...], preferred_element_type=f32)
                hid = (gate * jax.nn.sigmoid(gate) * up).astype(bf16)
                y_ref[...] = jnp.dot(hid, wd_bf[...], preferred_element_type=f32)

    @pl.when(i == n_active)
    def _():
        _wait_rows(tm, h_hbm, buf.at[slot], sem.at[slot])

    @pl.when(i >= n_active)
    def _():
        y_ref[...] = jnp.zeros_like(y_ref)


def _moe_stage(h2, tok_of_pos, tile_expert, n_active, w_gate, w_up, w_down):
    t, d = h2.shape
    n_tiles, _, tm = tok_of_pos.shape
    f = w_gate.shape[-1]
    wg = w_gate.reshape(N_EXPERTS, d, f)
    wu = w_up.reshape(N_EXPERTS, d, f)
    wd = w_down.reshape(N_EXPERTS, f, d)
    grid_spec = pltpu.PrefetchScalarGridSpec(
        num_scalar_prefetch=2,
        grid=(n_tiles,),
        in_specs=[
            pl.BlockSpec((1, 1, tm), lambda i, te, na: (i, 0, 0), memory_space=pltpu.SMEM),
            pl.BlockSpec((1, 1, tm), lambda i, te, na: (jnp.minimum(i + 1, n_tiles - 1), 0, 0),
                         memory_space=pltpu.SMEM),
            pl.BlockSpec(memory_space=pl.ANY),
            pl.BlockSpec((1, d, f), lambda i, te, na: (te[i], 0, 0)),
            pl.BlockSpec((1, d, f), lambda i, te, na: (te[i], 0, 0)),
            pl.BlockSpec((1, f, d), lambda i, te, na: (te[i], 0, 0)),
        ],
        out_specs=pl.BlockSpec((tm, d), lambda i, te, na: (i, 0)),
        scratch_shapes=[
            pltpu.VMEM((2, tm, d), jnp.float32), pltpu.SemaphoreType.DMA((2,)),
            pltpu.VMEM((d, f), jnp.bfloat16), pltpu.VMEM((d, f), jnp.bfloat16), pltpu.VMEM((f, d), jnp.bfloat16),
        ],
    )
    return pl.pallas_call(
        _moe_kernel,
        grid_spec=grid_spec,
        out_shape=jax.ShapeDtypeStruct((n_tiles * tm, d), jnp.float32),
        compiler_params=_params(("arbitrary",)),
    )(tile_expert, n_active, tok_of_pos, tok_of_pos, h2, wg, wu, wd)


def _combine_kernel(p1_ref, p2_ref, p1n_ref, p2n_ref, route_ref, x1_ref, gf_ref, y_hbm, o_ref, buf, sem,
                    *, final_norm):
    i = pl.program_id(0)
    n = pl.num_programs(0)
    tm = buf.shape[2]

    def gather(a_ref, b_ref, slot):
        _gather_rows(a_ref, tm, y_hbm, buf.at[slot, 0], sem.at[slot, 0])
        _gather_rows(b_ref, tm, y_hbm, buf.at[slot, 1], sem.at[slot, 1])

    @pl.when(i == 0)
    def _():
        gather(p1_ref, p2_ref, 0)

    @pl.when(i + 1 < n)
    def _():
        gather(p1n_ref, p2n_ref, (i + 1) % 2)

    slot = i % 2
    _wait_rows(tm, y_hbm, buf.at[slot, 0], sem.at[slot, 0])
    _wait_rows(tm, y_hbm, buf.at[slot, 1], sem.at[slot, 1])
    route = route_ref[...]
    w1 = route[:, ROUTE_W:ROUTE_W + 1]
    w2 = route[:, ROUTE_W + 1:ROUTE_W + 2]
    y = x1_ref[...] + (w1 * buf[slot, 0] + w2 * buf[slot, 1])
    if final_norm:
        y = y * lax.rsqrt(jnp.mean(y * y, axis=-1, keepdims=True) + EPS) * gf_ref[...]
    o_ref[...] = y


def _combine_stage(y_sorted, pos1, pos2, route, x1, g_final, final_norm, tm=256):
    t, d = x1.shape
    tm = min(tm, t)
    n = t // tm
    p1 = pos1.reshape(n, 1, tm)
    p2 = pos2.reshape(n, 1, tm)
    cur = lambda i: (i, 0, 0)
    nxt = lambda i: (jnp.minimum(i + 1, n - 1), 0, 0)
    smem = lambda index_map: pl.BlockSpec((1, 1, tm), index_map, memory_space=pltpu.SMEM)
    return pl.pallas_call(
        functools.partial(_combine_kernel, final_norm=final_norm),
        grid=(n,),
        in_specs=[
            smem(cur), smem(cur), smem(nxt), smem(nxt),
            pl.BlockSpec((tm, LANES), lambda i: (i, 0)),
            pl.BlockSpec((tm, d), lambda i: (i, 0)),
            pl.BlockSpec((1, d), lambda i: (0, 0)),
            pl.BlockSpec(memory_space=pl.ANY),
        ],
        out_specs=pl.BlockSpec((tm, d), lambda i: (i, 0)),
        out_shape=jax.ShapeDtypeStruct((t, d), jnp.float32),
        scratch_shapes=[pltpu.VMEM((2, 2, tm, d), jnp.float32), pltpu.SemaphoreType.DMA((2, 2))],
        compiler_params=_params(("arbitrary",)),
    )(p1, p2, p1, p2, route, x1, g_final.astype(jnp.float32)[None], y_sorted)


def _routing_tables(route, rank, count, tm):
    t = route.shape[0]
    e1 = route[:, ROUTE_ID].astype(jnp.int32)
    e2 = route[:, ROUTE_ID + 1].astype(jnp.int32)
    counts = count[0, ROUTE_E0:ROUTE_E0 + N_EXPERTS].astype(jnp.int32)
    padded = ((counts + tm - 1) // tm) * tm
    ends = jnp.cumsum(padded)
    offs = ends - padded
    pos1 = offs[e1] + rank[:, 0].astype(jnp.int32)
    pos2 = offs[e2] + rank[:, 1].astype(jnp.int32)
    n_tiles = (2 * t) // tm + N_EXPERTS
    tok = jnp.arange(t, dtype=jnp.int32)
    tok_of_pos = jnp.zeros((n_tiles * tm,), jnp.int32).at[jnp.concatenate([pos1, pos2])].set(
        jnp.concatenate([tok, tok]), unique_indices=True)
    tile_start = jnp.arange(n_tiles, dtype=jnp.int32) * tm
    tile_expert = jnp.minimum(jnp.sum((ends[None, :] <= tile_start[:, None]).astype(jnp.int32), axis=1),
                              N_EXPERTS - 1)
    n_active = (ends[-1] // tm).astype(jnp.int32).reshape(1)
    return pos1, pos2, tok_of_pos.reshape(n_tiles, 1, tm), tile_expert, n_active


def kernel(x, g_mix_norm, w_in, b_nsa_gate, cmp_pos_k, w_cmp_k1, w_cmp_k2, cmp_pos_v, w_cmp_v1, w_cmp_v2,
           w_alpha2, b_alpha, g_gla_norm, w_out, g_ffn_norm, w_router_group, b_router_group,
           w_router_expert, b_router_expert, w_expert_gate, w_expert_up, w_expert_down, g_final_norm):
    b, s, d = x.shape
    depth = w_in.shape[0]
    x2 = x.reshape(b * s, d)
    for l in range(depth):
        last = l == depth - 1
        w_big, w_small, col_scale, small_bias = _prep_in_proj_weights(w_in[l], b_nsa_gate[l])
        big, small = _in_proj(x2, g_mix_norm[l].astype(jnp.float32)[None], w_big, w_small, col_scale, small_bias)
        cmp = _compress_stage(big, b, s, cmp_pos_k[l], w_cmp_k1[l], w_cmp_k2[l],
                              cmp_pos_v[l], w_cmp_v1[l], w_cmp_v2[l])
        nsa = _nsa_stage(big, small, cmp, b, s).reshape(b * s, NSA_Q_WIDTH)
        gla = _gla_stage(big, small, w_alpha2[l], b_alpha[l], g_gla_norm[l], b, s).reshape(b * s, GLA_V_WIDTH)
        x1, h2, route = _out_proj_stage(x2, nsa, gla, w_out[l], g_ffn_norm[l], w_router_group[l],
                                        b_router_group[l], w_router_expert[l], b_router_expert[l])
        rank, count = _rank_stage(route)
        pos1, pos2, tok_of_pos, tile_expert, n_active = _routing_tables(route, rank, count, MOE_TM)
        y_sorted = _moe_stage(h2, tok_of_pos, tile_expert, n_active,
                              w_expert_gate[l], w_expert_up[l], w_expert_down[l])
        x2 = _combine_stage(y_sorted, pos1, pos2, route, x1, g_final_norm, last)
    return x2.reshape(b, s, d)
```

```python
import functools

import numpy as np
import jax
import jax.numpy as jnp
from jax import lax
from jax.experimental import pallas as pl
from jax.experimental.pallas import tpu as pltpu

NSA_HEADS = 8
NSA_KV_GROUPS = 2
HEADS_PER_GROUP = NSA_HEADS // NSA_KV_GROUPS
HEAD_DIM = 128
CMP_BLOCK = 32
CMP_STRIDE = 16
SEL_BLOCK = 64
SEL_TOPK = 8
N_LOCAL_BLOCKS = 2
WINDOW = 512
GLA_HEADS = 4
GLA_KEY_DIM = 128
GLA_VAL_DIM = 256
GATE_RANK = 16
GATE_TAU = 16.0
GLA_CHUNK = 64
GLA_SUB = 16
N_EXPERT_GROUPS = 4
EXPERTS_PER_GROUP = 8
N_EXPERTS = N_EXPERT_GROUPS * EXPERTS_PER_GROUP
EPS = 1e-6
MASK_VALUE = -1e30

NSA_Q_WIDTH = NSA_HEADS * HEAD_DIM
NSA_KV_WIDTH = NSA_KV_GROUPS * HEAD_DIM
GLA_QK_WIDTH = GLA_HEADS * GLA_KEY_DIM
GLA_V_WIDTH = GLA_HEADS * GLA_VAL_DIM

LANES = 128
VMEM_LIMIT_BYTES = 56 * 1024 * 1024

COL_NSA_Q = 0
COL_KV = COL_NSA_Q + NSA_Q_WIDTH
COL_GLA_Q = COL_KV + 6 * NSA_KV_WIDTH
COL_GLA_K = COL_GLA_Q + GLA_QK_WIDTH
COL_GLA_V = COL_GLA_K + GLA_QK_WIDTH
COL_GLA_GATE = COL_GLA_V + GLA_V_WIDTH
BIG_WIDTH = COL_GLA_GATE + GLA_V_WIDTH
SMALL_GATE = 0
SMALL_ALPHA = 3 * NSA_HEADS
SMALL_WIDTH = LANES


def _params(semantics):
    return pltpu.CompilerParams(dimension_semantics=semantics, vmem_limit_bytes=VMEM_LIMIT_BYTES)


def _split3(a):
    hi = a.astype(jnp.bfloat16)
    r1 = a - hi.astype(jnp.float32)
    mid = r1.astype(jnp.bfloat16)
    lo = (r1 - mid.astype(jnp.float32)).astype(jnp.bfloat16)
    return hi, mid, lo


def _in_proj_kernel(x_ref, g_ref, wb_ref, ws_ref, scale_ref, sbias_ref, big_ref, small_ref, h_ref):
    n = pl.program_id(1)

    @pl.when(n == 0)
    def _():
        xf = x_ref[...]
        y = xf * lax.rsqrt(jnp.mean(xf * xf, axis=-1, keepdims=True) + EPS)
        h = (y * g_ref[...]).astype(jnp.bfloat16)
        h_ref[...] = h
        small_ref[...] = jnp.dot(h, ws_ref[...], preferred_element_type=jnp.float32) + sbias_ref[...]

    acc = jnp.dot(h_ref[...], wb_ref[...], preferred_element_type=jnp.float32)
    big_ref[...] = (acc * scale_ref[...]).astype(big_ref.dtype)


def _in_proj(x2, gain, w_big, w_small, col_scale, small_bias, tm=1024, tn=1408):
    t, d = x2.shape
    tm = min(tm, t)
    grid = (t // tm, BIG_WIDTH // tn)
    return pl.pallas_call(
        _in_proj_kernel,
        grid=grid,
        in_specs=[
            pl.BlockSpec((tm, d), lambda m, n: (m, 0)),
            pl.BlockSpec((1, d), lambda m, n: (0, 0)),
            pl.BlockSpec((d, tn), lambda m, n: (0, n)),
            pl.BlockSpec((d, SMALL_WIDTH), lambda m, n: (0, 0)),
            pl.BlockSpec((1, tn), lambda m, n: (0, n)),
            pl.BlockSpec((1, SMALL_WIDTH), lambda m, n: (0, 0)),
        ],
        out_specs=[
            pl.BlockSpec((tm, tn), lambda m, n: (m, n)),
            pl.BlockSpec((tm, SMALL_WIDTH), lambda m, n: (m, 0)),
        ],
        out_shape=[
            jax.ShapeDtypeStruct((t, BIG_WIDTH), jnp.bfloat16),
            jax.ShapeDtypeStruct((t, SMALL_WIDTH), jnp.float32),
        ],
        scratch_shapes=[pltpu.VMEM((tm, d), jnp.bfloat16)],
        compiler_params=_params(("parallel", "arbitrary")),
    )(x2, gain, w_big, w_small, col_scale, small_bias)


def _prep_in_proj_weights(w_in, b_nsa_gate):
    sizes = (NSA_Q_WIDTH,) + (NSA_KV_WIDTH,) * 6 + (3 * NSA_HEADS, GLA_QK_WIDTH, GLA_QK_WIDTH,
                                                    GLA_V_WIDTH, GATE_RANK, GLA_V_WIDTH)
    offs = np.concatenate([[0], np.cumsum(sizes)])
    part = lambda i: w_in[:, offs[i]:offs[i + 1]]
    w_big = jnp.concatenate([part(i) for i in (0, 1, 2, 3, 4, 5, 6, 8, 9, 10, 12)], axis=1)
    pad = jnp.zeros((w_in.shape[0], SMALL_WIDTH - 3 * NSA_HEADS - GATE_RANK), w_in.dtype)
    w_small = jnp.concatenate([part(7), part(11), pad], axis=1)
    scale = np.ones((1, BIG_WIDTH), np.float32)
    scale[0, COL_NSA_Q:COL_NSA_Q + NSA_Q_WIDTH] = HEAD_DIM ** -0.5
    scale[0, COL_GLA_Q:COL_GLA_Q + GLA_QK_WIDTH] = GLA_KEY_DIM ** -0.5
    small_bias = jnp.concatenate([b_nsa_gate.astype(jnp.float32),
                                  jnp.zeros((SMALL_WIDTH - 3 * NSA_HEADS,), jnp.float32)])[None]
    return w_big.astype(jnp.bfloat16), w_small.astype(jnp.bfloat16), jnp.asarray(scale), small_bias


def _gelu_tanh(a):
    return 0.5 * a * (1.0 + jnp.tanh(np.sqrt(2.0 / np.pi) * (a + 0.044715 * (a * a * a))))


def _compress_kernel(z_ref, pos_ref, w1_ref, w2_ref, o_ref):
    z = z_ref[0].astype(jnp.float32)
    pos = pos_ref[0]
    z_lo = (z + pos[0:1]).astype(jnp.bfloat16)
    z_hi = (z + pos[1:2]).astype(jnp.bfloat16)
    u = jnp.dot(z_lo, w1_ref[0, 0], preferred_element_type=jnp.float32)
    v = jnp.dot(z_hi, w1_ref[0, 1], preferred_element_type=jnp.float32)
    a = u + pltpu.roll(v, v.shape[0] - 1, axis=0)
    hid = _gelu_tanh(a).astype(jnp.bfloat16)
    o_ref[0] = jnp.dot(hid, w2_ref[0], preferred_element_type=jnp.float32).astype(o_ref.dtype)


def _compress(z, pos, w1, w2, tr):
    _, rows, width = z.shape
    hid = w1.shape[-1]
    return pl.pallas_call(
        _compress_kernel,
        grid=(2, rows // tr),
        in_specs=[
            pl.BlockSpec((1, tr, width), lambda c, r: (c, r, 0)),
            pl.BlockSpec((1, 2, width), lambda c, r: (c, 0, 0)),
            pl.BlockSpec((1, 2, width, hid), lambda c, r: (c, 0, 0, 0)),
            pl.BlockSpec((1, hid, HEAD_DIM), lambda c, r: (c, 0, 0)),
        ],
        out_specs=pl.BlockSpec((1, tr, HEAD_DIM), lambda c, r: (c, r, 0)),
        out_shape=jax.ShapeDtypeStruct((2, rows, HEAD_DIM), jnp.bfloat16),
        compiler_params=_params(("parallel", "parallel")),
    )(z, pos, w1, w2)


def _compress_stage(big, b, s, cmp_pos_k, w_cmp_k1, w_cmp_k2, cmp_pos_v, w_cmp_v1, w_cmp_v2):
    g, dk = NSA_KV_GROUPS, HEAD_DIM
    n_seg = s // CMP_STRIDE
    kv = big[:, COL_KV:COL_KV + 2 * NSA_KV_WIDTH].reshape(b, n_seg, CMP_STRIDE, 2, g, dk)
    z = kv.transpose(3, 0, 4, 1, 2, 5).reshape(2, b * g * n_seg, CMP_STRIDE * dk)
    pos = jnp.stack([cmp_pos_k, cmp_pos_v]).reshape(2, 2, CMP_STRIDE * dk)
    w1 = jnp.stack([w_cmp_k1, w_cmp_v1]).reshape(2, 2, CMP_STRIDE * dk, -1).astype(jnp.bfloat16)
    w2 = jnp.stack([w_cmp_k2, w_cmp_v2]).astype(jnp.bfloat16)
    pairs = b * g
    per_tile = max(p for p in (1, 2, 4, 8) if pairs % p == 0 and p * n_seg <= 1024)
    out = _compress(z, pos, w1, w2, per_tile * n_seg)
    return out.reshape(2, b * g, n_seg, dk)


_NT = (((1,), (1,)), ((), ()))


def _nsa_kernel(slopes_ref, q_ref, kc_ref, vct_ref, ksel_ref, vselt_ref, kwin_ref, vwint_ref,
                gate_ref, ovl_ref, o_ref, selb_ref, alibi_ref, acc_ref, ml_ref, out_ref_t, *, tq, tk):
    g = pl.program_id(1)
    qi = pl.program_id(2)
    q0 = qi * tq
    n_seg = kc_ref.shape[1]
    n_sel = ovl_ref.shape[0]
    mxu_dtype = ksel_ref.dtype
    f32 = jnp.float32
    hd = HEAD_DIM

    t_row = q0 + lax.broadcasted_iota(jnp.int32, (1, tq), 1)

    cmp_end = lax.broadcasted_iota(jnp.int32, (n_seg, 1), 0) * CMP_STRIDE + (CMP_BLOCK - 1)
    valid_c = cmp_end <= t_row
    dist_c = (t_row - cmp_end).astype(f32)
    any_c = (t_row >= CMP_BLOCK - 1).astype(f32)
    kc = kc_ref[0]
    vct = vct_ref[0]
    psum = jnp.zeros((n_seg, tq), f32)
    heads = range(HEADS_PER_GROUP)
    slopes = [slopes_ref[g, hh] for hh in heads]
    gates = [jax.nn.sigmoid(gate_ref[0, 0, 3 * hh:3 * hh + 3, :]) for hh in heads]
    for hh in heads:
        qh = q_ref[0, :, hh * hd:(hh + 1) * hd]
        sc = lax.dot_general(kc, qh, _NT, preferred_element_type=f32)
        sc = jnp.where(valid_c, sc - slopes[hh] * dist_c, MASK_VALUE)
        m = jnp.max(sc, axis=0, keepdims=True)
        p = jnp.exp(sc - m)
        p = p * (any_c / jnp.sum(p, axis=0, keepdims=True))
        psum = psum + p
        o_cmp = jnp.dot(vct, p.astype(mxu_dtype), preferred_element_type=f32)
        out_ref_t[hh] = gates[hh][0:1] * o_cmp

    ovl = ovl_ref[...]
    imp = jnp.zeros((n_sel, tq), f32)
    for piece in _split3(psum):
        imp = imp + jnp.dot(ovl, piece, preferred_element_type=f32)
    blk = lax.broadcasted_iota(jnp.int32, (n_sel, 1), 0)
    cur = t_row // SEL_BLOCK
    causal_blk = blk * SEL_BLOCK <= t_row
    forced = (blk == 0) | ((blk <= cur) & (blk > cur - N_LOCAL_BLOCKS))
    imp = jnp.where(forced, -MASK_VALUE, jnp.where(causal_blk, imp, MASK_VALUE))
    cnt = jnp.zeros((n_sel, tq), f32)
    for i in range(n_sel):
        ri = imp[i:i + 1, :]
        beats = jnp.where(blk > i, (ri >= imp).astype(f32), (ri > imp).astype(f32))
        cnt = cnt + beats
    topk = min(SEL_TOPK, n_sel)
    selb_ref[...] = jnp.where(cnt < topk, 0.0, MASK_VALUE)

    d0 = (lax.broadcasted_iota(jnp.int32, (tk, tq), 1)
          - lax.broadcasted_iota(jnp.int32, (tk, tq), 0))
    d0f = d0.astype(f32)
    for hh in heads:
        alibi_ref[hh] = slopes[hh] * d0f
    blocks_per_chunk = tk // SEL_BLOCK

    def reset():
        acc_ref[...] = jnp.zeros_like(acc_ref)
        for hh in heads:
            ml_ref[2 * hh, 0:1, :] = jnp.full((1, tq), MASK_VALUE, f32)
            ml_ref[2 * hh + 1, 0:1, :] = jnp.zeros((1, tq), f32)

    def chunk(k_ref, vt_ref, k0, bias, delta_f):
        kblk = k_ref[0, pl.ds(k0, tk), :]
        vt = vt_ref[0, :, pl.ds(k0, tk)]
        scores = [lax.dot_general(kblk, q_ref[0, :, hh * hd:(hh + 1) * hd], _NT, preferred_element_type=f32)
                  for hh in heads]
        for hh in heads:
            s = scores[hh] - alibi_ref[hh]
            if bias is not None:
                s = s + bias
            cst = -slopes[hh] * delta_f
            m = ml_ref[2 * hh, 0:1, :]
            l = ml_ref[2 * hh + 1, 0:1, :]
            m_new = jnp.maximum(m, jnp.max(s, axis=0, keepdims=True) + cst)
            alpha = jnp.exp(m - m_new)
            p = jnp.exp(s - (m_new - cst))
            ml_ref[2 * hh, 0:1, :] = m_new
            ml_ref[2 * hh + 1, 0:1, :] = alpha * l + jnp.sum(p, axis=0, keepdims=True)
            acc_ref[hh] = alpha * acc_ref[hh] + jnp.dot(vt, p.astype(mxu_dtype), preferred_element_type=f32)

    def finish(branch):
        for hh in heads:
            out_ref_t[hh] += gates[hh][branch:branch + 1] * (acc_ref[hh] / ml_ref[2 * hh + 1, 0:1, :])

    def sel_rows(c):
        rows = [jnp.broadcast_to(selb_ref[pl.ds(c * blocks_per_chunk + j, 1), :], (SEL_BLOCK, tq))
                for j in range(blocks_per_chunk)]
        return jnp.concatenate(rows, axis=0)

    reset()

    def sel_body(c, carry):
        k0 = pl.multiple_of(c * tk, tk)
        chunk(ksel_ref, vselt_ref, k0, sel_rows(c), (q0 - k0).astype(f32))
        return carry

    lax.fori_loop(0, qi, sel_body, 0)
    k_diag = pl.multiple_of(q0, tk)
    chunk(ksel_ref, vselt_ref, k_diag, sel_rows(qi) + jnp.where(d0 >= 0, 0.0, MASK_VALUE), 0.0)
    finish(1)

    reset()
    for delta in range(0, WINDOW + tk, tk):
        lo_ok = delta - (tk - 1) >= 0
        hi_ok = delta + (tq - 1) < WINDOW
        if lo_ok and hi_ok:
            bias = None
        else:
            dist = d0 + delta
            bias = jnp.where(dist >= 0, jnp.where(dist < WINDOW, 0.0, MASK_VALUE), MASK_VALUE)
        if delta == 0:
            chunk(kwin_ref, vwint_ref, k_diag, bias, 0.0)
        else:
            @pl.when(q0 >= delta)
            def _(delta=delta, bias=bias):
                chunk(kwin_ref, vwint_ref, pl.multiple_of(q0 - delta, tk), bias, float(delta))
    finish(2)

    for hh in heads:
        o_ref[0, :, hh * hd:(hh + 1) * hd] = out_ref_t[hh].T.astype(o_ref.dtype)


def _nsa_stage(big, small, cmp, b, s, tq=256):
    g, hd = NSA_KV_GROUPS, HEAD_DIM
    tq = min(tq, s)
    tk = tq
    n_seg = s // CMP_STRIDE
    n_sel = s // SEL_BLOCK
    big3 = big.reshape(b, s, BIG_WIDTH)
    kv_col = lambda which: (COL_KV + which * NSA_KV_WIDTH) // hd
    vt = lambda which: big3[:, :, COL_KV + which * NSA_KV_WIDTH:COL_KV + (which + 1) * NSA_KV_WIDTH] \
        .reshape(b, s, g, hd).transpose(0, 2, 3, 1).reshape(b * g, hd, s)
    vsel_t, vwin_t = vt(3), vt(5)
    kc = cmp[0]
    vc_t = cmp[1].transpose(0, 2, 1)
    gate_t = small[:, SMALL_GATE:SMALL_GATE + 3 * NSA_HEADS].reshape(b, s, g, 3 * HEADS_PER_GROUP) \
        .transpose(0, 2, 3, 1)
    c_start = np.arange(n_seg)[None, :] * CMP_STRIDE
    s_start = np.arange(n_sel)[:, None] * SEL_BLOCK
    n_cmp = (s - CMP_BLOCK) // CMP_STRIDE + 1
    ovl = ((c_start < s_start + SEL_BLOCK) & (c_start + CMP_BLOCK > s_start)
           & (np.arange(n_seg)[None, :] < n_cmp)).astype(np.float32)
    ovl = jnp.asarray(ovl, jnp.bfloat16)
    slopes = jnp.asarray((2.0 ** (-8.0 * np.arange(1, NSA_HEADS + 1) / NSA_HEADS))
                         .reshape(g, HEADS_PER_GROUP), jnp.float32)
    gw = 3 * HEADS_PER_GROUP
    kernel = functools.partial(_nsa_kernel, tq=tq, tk=tk)
    return pl.pallas_call(
        kernel,
        grid=(b, g, s // tq),
        in_specs=[
            pl.BlockSpec(memory_space=pltpu.SMEM),
            pl.BlockSpec((1, tq, HEADS_PER_GROUP * hd), lambda bi, gi, qi: (bi, qi, gi)),
            pl.BlockSpec((1, n_seg, hd), lambda bi, gi, qi: (bi * NSA_KV_GROUPS + gi, 0, 0)),
            pl.BlockSpec((1, hd, n_seg), lambda bi, gi, qi: (bi * NSA_KV_GROUPS + gi, 0, 0)),
            pl.BlockSpec((1, s, hd), lambda bi, gi, qi: (bi, 0, kv_col(2) + gi)),
            pl.BlockSpec((1, hd, s), lambda bi, gi, qi: (bi * NSA_KV_GROUPS + gi, 0, 0)),
            pl.BlockSpec((1, s, hd), lambda bi, gi, qi: (bi, 0, kv_col(4) + gi)),
            pl.BlockSpec((1, hd, s), lambda bi, gi, qi: (bi * NSA_KV_GROUPS + gi, 0, 0)),
            pl.BlockSpec((1, 1, gw, tq), lambda bi, gi, qi: (bi, gi, 0, qi)),
            pl.BlockSpec((n_sel, n_seg), lambda bi, gi, qi: (0, 0)),
        ],
        out_specs=pl.BlockSpec((1, tq, HEADS_PER_GROUP * hd), lambda bi, gi, qi: (bi, qi, gi)),
        out_shape=jax.ShapeDtypeStruct((b, s, NSA_Q_WIDTH), big.dtype),
        scratch_shapes=[
            pltpu.VMEM((n_sel, tq), jnp.float32),
            pltpu.VMEM((HEADS_PER_GROUP, tk, tq), jnp.float32),
            pltpu.VMEM((HEADS_PER_GROUP, hd, tq), jnp.float32),
            pltpu.VMEM((2 * HEADS_PER_GROUP, 8, tq), jnp.float32),
            pltpu.VMEM((HEADS_PER_GROUP, hd, tq), jnp.float32),
        ],
        compiler_params=_params(("parallel", "parallel", "arbitrary")),
    )(slopes, big3, kc, vc_t, big3, vsel_t, big3, vwin_t, gate_t, ovl)


_TN = (((0,), (0,)), ((), ()))
_HI = lax.Precision.HIGHEST


def _gla_kernel(q_ref, k_ref, v_ref, og_ref, sm_ref, wa_ref, ba_ref, gn_ref, selw_ref, o_ref,
                state_ref, *, n_chunks):
    f32 = jnp.float32
    mxu_dtype = v_ref.dtype
    c_len, sub = GLA_CHUNK, GLA_SUB
    ns = c_len // sub
    dk = GLA_KEY_DIM

    @pl.when(pl.program_id(2) == 0)
    def _():
        state_ref[...] = jnp.zeros_like(state_ref)

    tc = n_chunks * c_len
    chunks = range(n_chunks)
    row = lax.broadcasted_iota(jnp.int32, (c_len, 1), 0)
    col = lax.broadcasted_iota(jnp.int32, (1, c_len), 1)
    sub_row = row // sub
    sub_col = col // sub
    t_loc = lax.broadcasted_iota(jnp.int32, (1, sub, 1), 1)
    neg_inf = -jnp.inf

    logits = jnp.dot(sm_ref[0], wa_ref[0], precision=_HI, preferred_element_type=f32) + ba_ref[0]
    glog = (jnp.minimum(logits, 0.0) - jnp.log(1.0 + jnp.exp(-jnp.abs(logits)))) * (1.0 / GATE_TAU)
    row_b = lax.broadcasted_iota(jnp.int32, (tc, 1), 0)
    col_b = lax.broadcasted_iota(jnp.int32, (1, tc), 1)
    block_tril = jnp.where(col_b <= row_b, (row_b // c_len == col_b // c_len).astype(f32), 0.0)
    block_tril = block_tril.astype(jnp.bfloat16)
    bcum_all = jnp.zeros((tc, dk), f32)
    for piece in _split3(glog):
        bcum_all = bcum_all + jnp.dot(block_tril, piece, preferred_element_type=f32)

    rows = [slice(c * c_len, (c + 1) * c_len) for c in chunks]
    qs = [q_ref[0, rows[c], :].astype(f32) for c in chunks]
    ks = [k_ref[0, rows[c], :].astype(f32) for c in chunks]
    vs = [v_ref[0, rows[c], :] for c in chunks]
    bcums = [bcum_all[rows[c], :] for c in chunks]
    b_lasts = [b[c_len - 1:c_len, :] for b in bcums]

    q_ins, kfs, qfs, ecats, kds = [], [], [], [], []
    for c in chunks:
        q, k, bcum = qs[c], ks[c], bcums[c]
        q_ins.append((q * jnp.exp(bcum)).astype(mxu_dtype))
        r = [bcum[(j + 1) * sub - 1:(j + 1) * sub, :] for j in range(ns)]
        r_rows = jnp.concatenate([jnp.broadcast_to(rj, (sub, dk)) for rj in r], axis=0)
        kfs.append((k * jnp.exp(r_rows - bcum)).astype(mxu_dtype))
        qfs.append([(q * jnp.exp(jnp.where(row >= (j + 1) * sub, bcum - r[j], neg_inf))).astype(mxu_dtype)
                    for j in range(ns - 1)])
        q3 = q.reshape(ns, sub, dk)
        k3 = k.reshape(ns, sub, dk)
        b3 = bcum.reshape(ns, sub, dk)
        pieces = []
        for s_ in range(sub):
            arg = jnp.where(t_loc >= s_, b3 - b3[:, s_:s_ + 1, :], neg_inf)
            e = q3 * k3[:, s_:s_ + 1, :] * jnp.exp(arg)
            pieces.append(e.reshape(c_len, dk).astype(mxu_dtype))
        ecats.append(jnp.concatenate(pieces, axis=1))
        kds.append((k * jnp.exp(b_lasts[c] - bcum)).astype(mxu_dtype))

    ajs = [[lax.dot_general(qfs[c][j], kfs[c], _NT, preferred_element_type=f32) for j in range(ns - 1)]
           for c in chunks]
    d_wides = [jnp.dot(ecats[c], selw_ref[...], preferred_element_type=f32) for c in chunks]
    updates = [lax.dot_general(vs[c], kds[c], _TN, preferred_element_type=f32) for c in chunks]
    o_intras = []
    for c in chunks:
        a = jnp.where(sub_row == sub_col, d_wides[c], 0.0)
        for j in range(ns - 1):
            a = a + jnp.where(sub_col == j, ajs[c][j], 0.0)
        o_intras.append(jnp.dot(a.astype(mxu_dtype), vs[c], preferred_element_type=f32))

    st = state_ref[...]
    for c in chunks:
        o = o_intras[c] + lax.dot_general(q_ins[c], st.astype(mxu_dtype), _NT, preferred_element_type=f32)
        st = st * jnp.exp(b_lasts[c]) + updates[c]
        rms = lax.rsqrt(jnp.mean(o * o, axis=-1, keepdims=True) + EPS)
        gate = og_ref[0, rows[c], :].astype(f32)
        y = o * rms * gn_ref[...] * (gate * jax.nn.sigmoid(gate))
        o_ref[0, rows[c], :] = y.astype(o_ref.dtype)
    state_ref[...] = st


def _gla_stage(big, small, w_alpha2, b_alpha, g_norm, b, s, tc=512):
    h, dk, dv = GLA_HEADS, GLA_KEY_DIM, GLA_VAL_DIM
    tc = min(tc, s)
    big3 = big.reshape(b, s, BIG_WIDTH)
    small3 = small.reshape(b, s, SMALL_WIDTH)
    wa = jnp.zeros((h, SMALL_WIDTH, dk), jnp.float32).at[:, SMALL_ALPHA:SMALL_ALPHA + GATE_RANK, :].set(
        w_alpha2.astype(jnp.float32).reshape(GATE_RANK, h, dk).transpose(1, 0, 2))
    ba = b_alpha.astype(jnp.float32).reshape(h, 1, dk)
    gn = g_norm.astype(jnp.float32).reshape(1, dv)
    sel = (np.arange(GLA_SUB * dk)[:, None] // dk == np.arange(GLA_CHUNK)[None, :] % GLA_SUB)
    selw = jnp.asarray(sel.astype(np.float32), big.dtype)
    kernel = functools.partial(_gla_kernel, n_chunks=tc // GLA_CHUNK)
    return pl.pallas_call(
        kernel,
        grid=(b, h, s // tc),
        in_specs=[
            pl.BlockSpec((1, tc, dk), lambda bi, hi, ci: (bi, ci, COL_GLA_Q // dk + hi)),
            pl.BlockSpec((1, tc, dk), lambda bi, hi, ci: (bi, ci, COL_GLA_K // dk + hi)),
            pl.BlockSpec((1, tc, dv), lambda bi, hi, ci: (bi, ci, COL_GLA_V // dv + hi)),
            pl.BlockSpec((1, tc, dv), lambda bi, hi, ci: (bi, ci, COL_GLA_GATE // dv + hi)),
            pl.BlockSpec((1, tc, SMALL_WIDTH), lambda bi, hi, ci: (bi, ci, 0)),
            pl.BlockSpec((1, SMALL_WIDTH, dk), lambda bi, hi, ci: (hi, 0, 0)),
            pl.BlockSpec((1, 1, dk), lambda bi, hi, ci: (hi, 0, 0)),
            pl.BlockSpec((1, dv), lambda bi, hi, ci: (0, 0)),
            pl.BlockSpec((GLA_SUB * dk, GLA_CHUNK), lambda bi, hi, ci: (0, 0)),
        ],
        out_specs=pl.BlockSpec((1, tc, dv), lambda bi, hi, ci: (bi, ci, hi)),
        out_shape=jax.ShapeDtypeStruct((b, s, GLA_V_WIDTH), big.dtype),
        scratch_shapes=[pltpu.VMEM((dv, dk), jnp.float32)],
        compiler_params=_params(("parallel", "parallel", "arbitrary")),
    )(big3, big3, big3, big3, small3, wa, ba, gn, selw)


ROUTE_E0 = N_EXPERT_GROUPS
ROUTE_ID = ROUTE_E0 + N_EXPERTS
ROUTE_W = ROUTE_ID + 2
MOE_TM = 256
GATHER_UNROLL = 8


def _out_proj_kernel(x_ref, nsa_ref, gla_ref, wt_ref, wb_ref, g_ref, wr_ref, br_ref,
                     x1_ref, h_ref, comb_ref):
    f32 = jnp.float32
    acc = jnp.dot(nsa_ref[...], wt_ref[...], preferred_element_type=f32)
    acc = acc + jnp.dot(gla_ref[...], wb_ref[...], preferred_element_type=f32)
    x1 = x_ref[...] + acc
    x1_ref[...] = x1
    hf = x1 * lax.rsqrt(jnp.mean(x1 * x1, axis=-1, keepdims=True) + EPS) * g_ref[...]
    h_ref[...] = hf.astype(h_ref.dtype)

    h_hi, h_mid, _ = _split3(hf)
    logits = (jnp.dot(h_hi, wr_ref[0], preferred_element_type=f32)
              + jnp.dot(h_hi, wr_ref[1], preferred_element_type=f32)
              + jnp.dot(h_mid, wr_ref[0], preferred_element_type=f32)) + br_ref[...]

    lane = lax.broadcasted_iota(jnp.int32, (1, LANES), 1).astype(f32)
    big_lane = float(LANES)
    neg_inf = -jnp.inf
    first_argmax = lambda vals, vmax: jnp.min(jnp.where(vals == vmax, lane, big_lane), axis=-1, keepdims=True)
    gl = jnp.where(lane < N_EXPERT_GROUPS, logits, neg_inf)
    gmax = jnp.max(gl, axis=-1, keepdims=True)
    gsel = first_argmax(gl, gmax)
    gw = 1.0 / jnp.sum(jnp.exp(gl - gmax), axis=-1, keepdims=True)
    lo = ROUTE_E0 + EXPERTS_PER_GROUP * gsel
    el = jnp.where(lane >= lo, jnp.where(lane < lo + EXPERTS_PER_GROUP, logits, neg_inf), neg_inf)
    m1 = jnp.max(el, axis=-1, keepdims=True)
    i1 = first_argmax(el, m1)
    el2 = jnp.where(lane == i1, neg_inf, el)
    m2 = jnp.max(el2, axis=-1, keepdims=True)
    i2 = first_argmax(el2, m2)
    e2 = jnp.exp(m2 - m1)
    w1 = gw / (1.0 + e2)
    w2 = w1 * e2
    onehot = jnp.where(lane == i1, 1.0, 0.0) + jnp.where(lane == i2, 1.0, 0.0)
    meta = (jnp.where(lane == ROUTE_ID, i1 - ROUTE_E0, 0.0) + jnp.where(lane == ROUTE_ID + 1, i2 - ROUTE_E0, 0.0)
            + jnp.where(lane == ROUTE_W, w1, 0.0) + jnp.where(lane == ROUTE_W + 1, w2, 0.0))
    comb_ref[...] = onehot + meta


def _out_proj_stage(x2, nsa, gla, w_out, g_ffn, w_rg, b_rg, w_re, b_re, tm=512):
    t, d = x2.shape
    tm = min(tm, t)
    half = nsa.shape[1]
    w_top = w_out[:half].astype(nsa.dtype)
    w_bot = w_out[half:].astype(nsa.dtype)
    wr = jnp.concatenate([w_rg, w_re.reshape(d, N_EXPERTS),
                          jnp.zeros((d, LANES - ROUTE_E0 - N_EXPERTS), jnp.float32)], axis=1)
    wr_hi, wr_mid, _ = _split3(wr)
    wr2 = jnp.stack([wr_hi, wr_mid])
    br = jnp.concatenate([b_rg, b_re.reshape(N_EXPERTS),
                          jnp.zeros((LANES - ROUTE_E0 - N_EXPERTS,), jnp.float32)])[None]
    row = lambda m: (m, 0)
    fixed = lambda m: (0, 0)
    return pl.pallas_call(
        _out_proj_kernel,
        grid=(t // tm,),
        in_specs=[
            pl.BlockSpec((tm, d), row),
            pl.BlockSpec((tm, half), row),
            pl.BlockSpec((tm, half), row),
            pl.BlockSpec((half, d), fixed),
            pl.BlockSpec((half, d), fixed),
            pl.BlockSpec((1, d), fixed),
            pl.BlockSpec((2, d, LANES), lambda m: (0, 0, 0)),
            pl.BlockSpec((1, LANES), fixed),
        ],
        out_specs=[pl.BlockSpec((tm, d), row), pl.BlockSpec((tm, d), row), pl.BlockSpec((tm, LANES), row)],
        out_shape=[
            jax.ShapeDtypeStruct((t, d), jnp.float32),
            jax.ShapeDtypeStruct((t, d), jnp.float32),
            jax.ShapeDtypeStruct((t, LANES), jnp.float32),
        ],
        compiler_params=_params(("parallel",)),
    )(x2, nsa, gla, w_top, w_bot, g_ffn.astype(jnp.float32)[None], wr2, br)


def _rank_kernel(route_ref, rank_ref, count_ref, carry_ref):
    f32 = jnp.float32
    tm = route_ref.shape[0]

    @pl.when(pl.program_id(0) == 0)
    def _():
        carry_ref[...] = jnp.zeros_like(carry_ref)

    route = route_ref[...]
    lane = lax.broadcasted_iota(jnp.int32, (1, LANES), 1)
    lane_f = lane.astype(f32)
    onehot = jnp.where((lane >= ROUTE_E0) & (lane < ROUTE_ID), route, 0.0)
    row = lax.broadcasted_iota(jnp.int32, (tm, tm), 0)
    col = lax.broadcasted_iota(jnp.int32, (tm, tm), 1)
    strict_lower = (col < row).astype(jnp.bfloat16)
    before = jnp.dot(strict_lower, onehot.astype(jnp.bfloat16), preferred_element_type=f32) + carry_ref[0:1, :]
    ranks = []
    for j in range(2):
        lane_j = route[:, ROUTE_ID + j:ROUTE_ID + j + 1] + float(ROUTE_E0)
        ranks.append(jnp.sum(jnp.where(lane_f == lane_j, before, 0.0), axis=-1, keepdims=True))
    rank_ref[...] = jnp.where(lane == 0, ranks[0], jnp.where(lane == 1, ranks[1], 0.0))
    total = carry_ref[0:1, :] + jnp.sum(onehot, axis=0, keepdims=True)
    carry_ref[...] = jnp.broadcast_to(total, carry_ref.shape)
    count_ref[...] = jnp.broadcast_to(total, count_ref.shape)


def _rank_stage(route, tm=512):
    t = route.shape[0]
    tm = min(tm, t)
    return pl.pallas_call(
        _rank_kernel,
        grid=(t // tm,),
        in_specs=[pl.BlockSpec((tm, LANES), lambda m: (m, 0))],
        out_specs=[pl.BlockSpec((tm, LANES), lambda m: (m, 0)), pl.BlockSpec((8, LANES), lambda m: (0, 0))],
        out_shape=[jax.ShapeDtypeStruct((t, LANES), jnp.float32), jax.ShapeDtypeStruct((8, LANES), jnp.float32)],
        scratch_shapes=[pltpu.VMEM((8, LANES), jnp.float32)],
        compiler_params=_params(("arbitrary",)),
    )(route)


def _gather_rows(idx_ref, n_rows, src_hbm, dst, sem):
    def body(r, carry):
        tok = idx_ref[0, 0, r]
        pltpu.make_async_copy(src_hbm.at[pl.ds(tok, 1), :], dst.at[pl.ds(r, 1), :], sem).start()
        return carry
    lax.fori_loop(0, n_rows, body, 0, unroll=GATHER_UNROLL)


def _wait_rows(n_rows, src_hbm, dst, sem):
    pltpu.make_async_copy(src_hbm.at[pl.ds(0, n_rows), :], dst, sem).wait()


def _moe_kernel(te_ref, na_ref, idx_ref, idx_next_ref, h_hbm, wg_ref, wu_ref, wd_ref, y_ref,
                buf, sem, wg_bf, wu_bf, wd_bf):
    f32 = jnp.float32
    bf16 = jnp.bfloat16
    i = pl.program_id(0)
    n_active = na_ref[0]
    tm = buf.shape[1]
    slot = i % 2

    @pl.when(i == 0)
    def _():
        _gather_rows(idx_ref, tm, h_hbm, buf.at[0], sem.at[0])

    @pl.when(i < n_active)
    def _():
        expert_changed = (i == 0) | (te_ref[i] != te_ref[jnp.maximum(i - 1, 0)])

        @pl.when(expert_changed)
        def _():
            wg_bf[...] = wg_ref[0].astype(bf16)
            wu_bf[...] = wu_ref[0].astype(bf16)
            wd_bf[...] = wd_ref[0].astype(bf16)

        for cur in range(2):
            @pl.when(slot == cur)
            def _(cur=cur):
                nxt = 1 - cur
                _wait_rows(tm, h_hbm, buf.at[cur], sem.at[cur])
                for r in range(tm):
                    tok = idx_next_ref[0, 0, r]
                    pltpu.make_async_copy(h_hbm.at[pl.ds(tok, 1), :], buf.at[nxt, pl.ds(r, 1), :],
                                          sem.at[nxt]).start()
                xb = buf[cur].astype(bf16)
                gate = jnp.dot(xb, wg_bf[...], preferred_element_type=f32)
                up = jnp.dot(xb, wu_bf[...], preferred_element_type=f32)
                hid = (gate * jax.nn.sigmoid(gate) * up).astype(bf16)
                y_ref[...] = jnp.dot(hid, wd_bf[...], preferred_element_type=f32)

    @pl.when(i == n_active)
    def _():
        _wait_rows(tm, h_hbm, buf.at[slot], sem.at[slot])

    @pl.when(i >= n_active)
    def _():
        y_ref[...] = jnp.zeros_like(y_ref)


def _moe_stage(h2, tok_of_pos, tile_expert, n_active, w_gate, w_up, w_down):
    t, d = h2.shape
    n_tiles, _, tm = tok_of_pos.shape
    f = w_gate.shape[-1]
    wg = w_gate.reshape(N_EXPERTS, d, f)
    wu = w_up.reshape(N_EXPERTS, d, f)
    wd = w_down.reshape(N_EXPERTS, f, d)
    grid_spec = pltpu.PrefetchScalarGridSpec(
        num_scalar_prefetch=2,
        grid=(n_tiles,),
        in_specs=[
            pl.BlockSpec((1, 1, tm), lambda i, te, na: (i, 0, 0), memory_space=pltpu.SMEM),
            pl.BlockSpec((1, 1, tm), lambda i, te, na: (jnp.minimum(i + 1, n_tiles - 1), 0, 0),
                         memory_space=pltpu.SMEM),
            pl.BlockSpec(memory_space=pl.ANY),
            pl.BlockSpec((1, d, f), lambda i, te, na: (te[i], 0, 0)),
            pl.BlockSpec((1, d, f), lambda i, te, na: (te[i], 0, 0)),
            pl.BlockSpec((1, f, d), lambda i, te, na: (te[i], 0, 0)),
        ],
        out_specs=pl.BlockSpec((tm, d), lambda i, te, na: (i, 0)),
        scratch_shapes=[
            pltpu.VMEM((2, tm, d), jnp.float32), pltpu.SemaphoreType.DMA((2,)),
            pltpu.VMEM((d, f), jnp.bfloat16), pltpu.VMEM((d, f), jnp.bfloat16), pltpu.VMEM((f, d), jnp.bfloat16),
        ],
    )
    return pl.pallas_call(
        _moe_kernel,
        grid_spec=grid_spec,
        out_shape=jax.ShapeDtypeStruct((n_tiles * tm, d), jnp.float32),
        compiler_params=_params(("arbitrary",)),
    )(tile_expert, n_active, tok_of_pos, tok_of_pos, h2, wg, wu, wd)


def _combine_kernel(p1_ref, p2_ref, p1n_ref, p2n_ref, route_ref, x1_ref, gf_ref, y_hbm, o_ref, buf, sem,
                    *, final_norm):
    i = pl.program_id(0)
    n = pl.num_programs(0)
    tm = buf.shape[2]

    def gather(a_ref, b_ref, slot):
        _gather_rows(a_ref, tm, y_hbm, buf.at[slot, 0], sem.at[slot, 0])
        _gather_rows(b_ref, tm, y_hbm, buf.at[slot, 1], sem.at[slot, 1])

    @pl.when(i == 0)
    def _():
        gather(p1_ref, p2_ref, 0)

    @pl.when(i + 1 < n)
    def _():
        gather(p1n_ref, p2n_ref, (i + 1) % 2)

    slot = i % 2
    _wait_rows(tm, y_hbm, buf.at[slot, 0], sem.at[slot, 0])
    _wait_rows(tm, y_hbm, buf.at[slot, 1], sem.at[slot, 1])
    route = route_ref[...]
    w1 = route[:, ROUTE_W:ROUTE_W + 1]
    w2 = route[:, ROUTE_W + 1:ROUTE_W + 2]
    y = x1_ref[...] + (w1 * buf[slot, 0] + w2 * buf[slot, 1])
    if final_norm:
        y = y * lax.rsqrt(jnp.mean(y * y, axis=-1, keepdims=True) + EPS) * gf_ref[...]
    o_ref[...] = y


def _combine_stage(y_sorted, pos1, pos2, route, x1, g_final, final_norm, tm=256):
    t, d = x1.shape
    tm = min(tm, t)
    n = t // tm
    p1 = pos1.reshape(n, 1, tm)
    p2 = pos2.reshape(n, 1, tm)
    cur = lambda i: (i, 0, 0)
    nxt = lambda i: (jnp.minimum(i + 1, n - 1), 0, 0)
    smem = lambda index_map: pl.BlockSpec((1, 1, tm), index_map, memory_space=pltpu.SMEM)
    return pl.pallas_call(
        functools.partial(_combine_kernel, final_norm=final_norm),
        grid=(n,),
        in_specs=[
            smem(cur), smem(cur), smem(nxt), smem(nxt),
            pl.BlockSpec((tm, LANES), lambda i: (i, 0)),
            pl.BlockSpec((tm, d), lambda i: (i, 0)),
            pl.BlockSpec((1, d), lambda i: (0, 0)),
            pl.BlockSpec(memory_space=pl.ANY),
        ],
        out_specs=pl.BlockSpec((tm, d), lambda i: (i, 0)),
        out_shape=jax.ShapeDtypeStruct((t, d), jnp.float32),
        scratch_shapes=[pltpu.VMEM((2, 2, tm, d), jnp.float32), pltpu.SemaphoreType.DMA((2, 2))],
        compiler_params=_params(("arbitrary",)),
    )(p1, p2, p1, p2, route, x1, g_final.astype(jnp.float32)[None], y_sorted)


def _routing_tables(route, rank, count, tm):
    t = route.shape[0]
    e1 = route[:, ROUTE_ID].astype(jnp.int32)
    e2 = route[:, ROUTE_ID + 1].astype(jnp.int32)
    counts = count[0, ROUTE_E0:ROUTE_E0 + N_EXPERTS].astype(jnp.int32)
    padded = ((counts + tm - 1) // tm) * tm
    ends = jnp.cumsum(padded)
    offs = ends - padded
    pos1 = offs[e1] + rank[:, 0].astype(jnp.int32)
    pos2 = offs[e2] + rank[:, 1].astype(jnp.int32)
    n_tiles = (2 * t) // tm + N_EXPERTS
    tok = jnp.arange(t, dtype=jnp.int32)
    tok_of_pos = jnp.zeros((n_tiles * tm,), jnp.int32).at[jnp.concatenate([pos1, pos2])].set(
        jnp.concatenate([tok, tok]), unique_indices=True)
    tile_start = jnp.arange(n_tiles, dtype=jnp.int32) * tm
    tile_expert = jnp.minimum(jnp.sum((ends[None, :] <= tile_start[:, None]).astype(jnp.int32), axis=1),
                              N_EXPERTS - 1)
    n_active = (ends[-1] // tm).astype(jnp.int32).reshape(1)
    return pos1, pos2, tok_of_pos.reshape(n_tiles, 1, tm), tile_expert, n_active


def kernel(x, g_mix_norm, w_in, b_nsa_gate, cmp_pos_k, w_cmp_k1, w_cmp_k2, cmp_pos_v, w_cmp_v1, w_cmp_v2,
           w_alpha2, b_alpha, g_gla_norm, w_out, g_ffn_norm, w_router_group, b_router_group,
           w_router_expert, b_router_expert, w_expert_gate, w_expert_up, w_expert_down, g_final_norm):
    b, s, d = x.shape
    depth = w_in.shape[0]
    x2 = x.reshape(b * s, d)
    for l in range(depth):
        last = l == depth - 1
        w_big, w_small, col_scale, small_bias = _prep_in_proj_weights(w_in[l], b_nsa_gate[l])
        big, small = _in_proj(x2, g_mix_norm[l].astype(jnp.float32)[None], w_big, w_small, col_scale, small_bias)
        cmp = _compress_stage(big, b, s, cmp_pos_k[l], w_cmp_k1[l], w_cmp_k2[l],
                              cmp_pos_v[l], w_cmp_v1[l], w_cmp_v2[l])
        nsa = _nsa_stage(big, small, cmp, b, s).reshape(b * s, NSA_Q_WIDTH)
        gla = _gla_stage(big, small, w_alpha2[l], b_alpha[l], g_gla_norm[l], b, s).reshape(b * s, GLA_V_WIDTH)
        x1, h2, route = _out_proj_stage(x2, nsa, gla, w_out[l], g_ffn_norm[l], w_router_group[l],
                                        b_router_group[l], w_router_expert[l], b_router_expert[l])
        rank, count = _rank_stage(route)
        pos1, pos2, tok_of_pos, tile_expert, n_active = _routing_tables(route, rank, count, MOE_TM)
        y_sorted = _moe_stage(h2, tok_of_pos, tile_expert, n_active,
                              w_expert_gate[l], w_expert_up[l], w_expert_down[l])
        x2 = _combine_stage(y_sorted, pos1, pos2, route, x1, g_final_norm, last)
    return x2.reshape(b, s, d)
```

```python
import functools

import numpy as np
import jax
import jax.numpy as jnp
from jax import lax
from jax.experimental import pallas as pl
from jax.experimental.pallas import tpu as pltpu

NSA_HEADS = 8
NSA_KV_GROUPS = 2
HEADS_PER_GROUP = NSA_HEADS // NSA_KV_GROUPS
HEAD_DIM = 128
CMP_BLOCK = 32
CMP_STRIDE = 16
SEL_BLOCK = 64
SEL_TOPK = 8
N_LOCAL_BLOCKS = 2
WINDOW = 512
GLA_HEADS = 4
GLA_KEY_DIM = 128
GLA_VAL_DIM = 256
GATE_RANK = 16
GATE_TAU = 16.0
GLA_CHUNK = 64
GLA_SUB = 16
N_EXPERT_GROUPS = 4
EXPERTS_PER_GROUP = 8
N_EXPERTS = N_EXPERT_GROUPS * EXPERTS_PER_GROUP
EPS = 1e-6
MASK_VALUE = -1e30

NSA_Q_WIDTH = NSA_HEADS * HEAD_DIM
NSA_KV_WIDTH = NSA_KV_GROUPS * HEAD_DIM
GLA_QK_WIDTH = GLA_HEADS * GLA_KEY_DIM
GLA_V_WIDTH = GLA_HEADS * GLA_VAL_DIM

LANES = 128
VMEM_LIMIT_BYTES = 56 * 1024 * 1024

COL_NSA_Q = 0
COL_KV = COL_NSA_Q + NSA_Q_WIDTH
COL_GLA_Q = COL_KV + 6 * NSA_KV_WIDTH
COL_GLA_K = COL_GLA_Q + GLA_QK_WIDTH
COL_GLA_V = COL_GLA_K + GLA_QK_WIDTH
COL_GLA_GATE = COL_GLA_V + GLA_V_WIDTH
BIG_WIDTH = COL_GLA_GATE + GLA_V_WIDTH
SMALL_GATE = 0
SMALL_ALPHA = 3 * NSA_HEADS
SMALL_WIDTH = LANES


def _params(semantics):
    return pltpu.CompilerParams(dimension_semantics=semantics, vmem_limit_bytes=VMEM_LIMIT_BYTES)


def _split3(a):
    hi = a.astype(jnp.bfloat16)
    r1 = a - hi.astype(jnp.float32)
    mid = r1.astype(jnp.bfloat16)
    lo = (r1 - mid.astype(jnp.float32)).astype(jnp.bfloat16)
    return hi, mid, lo


def _in_proj_kernel(x_ref, g_ref, wb_ref, ws_ref, scale_ref, sbias_ref, big_ref, small_ref, h_ref):
    n = pl.program_id(1)

    @pl.when(n == 0)
    def _():
        xf = x_ref[...]
        y = xf * lax.rsqrt(jnp.mean(xf * xf, axis=-1, keepdims=True) + EPS)
        h = (y * g_ref[...]).astype(jnp.bfloat16)
        h_ref[...] = h
        small_ref[...] = jnp.dot(h, ws_ref[...], preferred_element_type=jnp.float32) + sbias_ref[...]

    acc = jnp.dot(h_ref[...], wb_ref[...], preferred_element_type=jnp.float32)
    big_ref[...] = (acc * scale_ref[...]).astype(big_ref.dtype)


def _in_proj(x2, gain, w_big, w_small, col_scale, small_bias, tm=1024, tn=1408):
    t, d = x2.shape
    tm = min(tm, t)
    grid = (t // tm, BIG_WIDTH // tn)
    return pl.pallas_call(
        _in_proj_kernel,
        grid=grid,
        in_specs=[
            pl.BlockSpec((tm, d), lambda m, n: (m, 0)),
            pl.BlockSpec((1, d), lambda m, n: (0, 0)),
            pl.BlockSpec((d, tn), lambda m, n: (0, n)),
            pl.BlockSpec((d, SMALL_WIDTH), lambda m, n: (0, 0)),
            pl.BlockSpec((1, tn), lambda m, n: (0, n)),
            pl.BlockSpec((1, SMALL_WIDTH), lambda m, n: (0, 0)),
        ],
        out_specs=[
            pl.BlockSpec((tm, tn), lambda m, n: (m, n)),
            pl.BlockSpec((tm, SMALL_WIDTH), lambda m, n: (m, 0)),
        ],
        out_shape=[
            jax.ShapeDtypeStruct((t, BIG_WIDTH), jnp.bfloat16),
            jax.ShapeDtypeStruct((t, SMALL_WIDTH), jnp.float32),
        ],
        scratch_shapes=[pltpu.VMEM((tm, d), jnp.bfloat16)],
        compiler_params=_params(("parallel", "arbitrary")),
    )(x2, gain, w_big, w_small, col_scale, small_bias)


def _prep_in_proj_weights(w_in, b_nsa_gate):
    sizes = (NSA_Q_WIDTH,) + (NSA_KV_WIDTH,) * 6 + (3 * NSA_HEADS, GLA_QK_WIDTH, GLA_QK_WIDTH,
                                                    GLA_V_WIDTH, GATE_RANK, GLA_V_WIDTH)
    offs = np.concatenate([[0], np.cumsum(sizes)])
    part = lambda i: w_in[:, offs[i]:offs[i + 1]]
    w_big = jnp.concatenate([part(i) for i in (0, 1, 2, 3, 4, 5, 6, 8, 9, 10, 12)], axis=1)
    pad = jnp.zeros((w_in.shape[0], SMALL_WIDTH - 3 * NSA_HEADS - GATE_RANK), w_in.dtype)
    w_small = jnp.concatenate([part(7), part(11), pad], axis=1)
    scale = np.ones((1, BIG_WIDTH), np.float32)
    scale[0, COL_NSA_Q:COL_NSA_Q + NSA_Q_WIDTH] = HEAD_DIM ** -0.5
    scale[0, COL_GLA_Q:COL_GLA_Q + GLA_QK_WIDTH] = GLA_KEY_DIM ** -0.5
    small_bias = jnp.concatenate([b_nsa_gate.astype(jnp.float32),
                                  jnp.zeros((SMALL_WIDTH - 3 * NSA_HEADS,), jnp.float32)])[None]
    return w_big.astype(jnp.bfloat16), w_small.astype(jnp.bfloat16), jnp.asarray(scale), small_bias


def _gelu_tanh(a):
    return 0.5 * a * (1.0 + jnp.tanh(np.sqrt(2.0 / np.pi) * (a + 0.044715 * (a * a * a))))


def _compress_kernel(z_ref, pos_ref, w1_ref, w2_ref, o_ref):
    z = z_ref[0].astype(jnp.float32)
    pos = pos_ref[0]
    z_lo = (z + pos[0:1]).astype(jnp.bfloat16)
    z_hi = (z + pos[1:2]).astype(jnp.bfloat16)
    u = jnp.dot(z_lo, w1_ref[0, 0], preferred_element_type=jnp.float32)
    v = jnp.dot(z_hi, w1_ref[0, 1], preferred_element_type=jnp.float32)
    a = u + pltpu.roll(v, v.shape[0] - 1, axis=0)
    hid = _gelu_tanh(a).astype(jnp.bfloat16)
    o_ref[0] = jnp.dot(hid, w2_ref[0], preferred_element_type=jnp.float32).astype(o_ref.dtype)


def _compress(z, pos, w1, w2, tr):
    _, rows, width = z.shape
    hid = w1.shape[-1]
    return pl.pallas_call(
        _compress_kernel,
        grid=(2, rows // tr),
        in_specs=[
            pl.BlockSpec((1, tr, width), lambda c, r: (c, r, 0)),
            pl.BlockSpec((1, 2, width), lambda c, r: (c, 0, 0)),
            pl.BlockSpec((1, 2, width, hid), lambda c, r: (c, 0, 0, 0)),
            pl.BlockSpec((1, hid, HEAD_DIM), lambda c, r: (c, 0, 0)),
        ],
        out_specs=pl.BlockSpec((1, tr, HEAD_DIM), lambda c, r: (c, r, 0)),
        out_shape=jax.ShapeDtypeStruct((2, rows, HEAD_DIM), jnp.bfloat16),
        compiler_params=_params(("parallel", "parallel")),
    )(z, pos, w1, w2)


def _compress_stage(big, b, s, cmp_pos_k, w_cmp_k1, w_cmp_k2, cmp_pos_v, w_cmp_v1, w_cmp_v2):
    g, dk = NSA_KV_GROUPS, HEAD_DIM
    n_seg = s // CMP_STRIDE
    kv = big[:, COL_KV:COL_KV + 2 * NSA_KV_WIDTH].reshape(b, n_seg, CMP_STRIDE, 2, g, dk)
    z = kv.transpose(3, 0, 4, 1, 2, 5).reshape(2, b * g * n_seg, CMP_STRIDE * dk)
    pos = jnp.stack([cmp_pos_k, cmp_pos_v]).reshape(2, 2, CMP_STRIDE * dk)
    w1 = jnp.stack([w_cmp_k1, w_cmp_v1]).reshape(2, 2, CMP_STRIDE * dk, -1).astype(jnp.bfloat16)
    w2 = jnp.stack([w_cmp_k2, w_cmp_v2]).astype(jnp.bfloat16)
    pairs = b * g
    per_tile = max(p for p in (1, 2, 4, 8) if pairs % p == 0 and p * n_seg <= 1024)
    out = _compress(z, pos, w1, w2, per_tile * n_seg)
    return out.reshape(2, b * g, n_seg, dk)


_NT = (((1,), (1,)), ((), ()))


def _nsa_kernel(slopes_ref, q_ref, kc_ref, vct_ref, ksel_ref, vselt_ref, kwin_ref, vwint_ref,
                gate_ref, ovl_ref, o_ref, selb_ref, alibi_ref, acc_ref, ml_ref, out_ref_t, *, tq, tk):
    g = pl.program_id(1)
    qi = pl.program_id(2)
    q0 = qi * tq
    n_seg = kc_ref.shape[1]
    n_sel = ovl_ref.shape[0]
    mxu_dtype = ksel_ref.dtype
    f32 = jnp.float32
    hd = HEAD_DIM

    t_row = q0 + lax.broadcasted_iota(jnp.int32, (1, tq), 1)

    cmp_end = lax.broadcasted_iota(jnp.int32, (n_seg, 1), 0) * CMP_STRIDE + (CMP_BLOCK - 1)
    valid_c = cmp_end <= t_row
    dist_c = (t_row - cmp_end).astype(f32)
    any_c = (t_row >= CMP_BLOCK - 1).astype(f32)
    kc = kc_ref[0]
    vct = vct_ref[0]
    psum = jnp.zeros((n_seg, tq), f32)
    heads = range(HEADS_PER_GROUP)
    slopes = [slopes_ref[g, hh] for hh in heads]
    gates = [jax.nn.sigmoid(gate_ref[0, 0, 3 * hh:3 * hh + 3, :]) for hh in heads]
    for hh in heads:
        qh = q_ref[0, :, hh * hd:(hh + 1) * hd]
        sc = lax.dot_general(kc, qh, _NT, preferred_element_type=f32)
        sc = jnp.where(valid_c, sc - slopes[hh] * dist_c, MASK_VALUE)
        m = jnp.max(sc, axis=0, keepdims=True)
        p = jnp.exp(sc - m)
        p = p * (any_c / jnp.sum(p, axis=0, keepdims=True))
        psum = psum + p
        o_cmp = jnp.dot(vct, p.astype(mxu_dtype), preferred_element_type=f32)
        out_ref_t[hh] = gates[hh][0:1] * o_cmp

    ovl = ovl_ref[...]
    imp = jnp.zeros((n_sel, tq), f32)
    for piece in _split3(psum):
        imp = imp + jnp.dot(ovl, piece, preferred_element_type=f32)
    blk = lax.broadcasted_iota(jnp.int32, (n_sel, 1), 0)
    cur = t_row // SEL_BLOCK
    causal_blk = blk * SEL_BLOCK <= t_row
    forced = (blk == 0) | ((blk <= cur) & (blk > cur - N_LOCAL_BLOCKS))
    imp = jnp.where(forced, -MASK_VALUE, jnp.where(causal_blk, imp, MASK_VALUE))
    cnt = jnp.zeros((n_sel, tq), f32)
    for i in range(n_sel):
        ri = imp[i:i + 1, :]
        beats = jnp.where(blk > i, (ri >= imp).astype(f32), (ri > imp).astype(f32))
        cnt = cnt + beats
    topk = min(SEL_TOPK, n_sel)
    selb_ref[...] = jnp.where(cnt < topk, 0.0, MASK_VALUE)

    d0 = (lax.broadcasted_iota(jnp.int32, (tk, tq), 1)
          - lax.broadcasted_iota(jnp.int32, (tk, tq), 0))
    d0f = d0.astype(f32)
    for hh in heads:
        alibi_ref[hh] = slopes[hh] * d0f
    blocks_per_chunk = tk // SEL_BLOCK

    def reset():
        acc_ref[...] = jnp.zeros_like(acc_ref)
        for hh in heads:
            ml_ref[2 * hh, 0:1, :] = jnp.full((1, tq), MASK_VALUE, f32)
            ml_ref[2 * hh + 1, 0:1, :] = jnp.zeros((1, tq), f32)

    def chunk(k_ref, vt_ref, k0, bias, delta_f):
        kblk = k_ref[0, pl.ds(k0, tk), :]
        vt = vt_ref[0, :, pl.ds(k0, tk)]
        scores = [lax.dot_general(kblk, q_ref[0, :, hh * hd:(hh + 1) * hd], _NT, preferred_element_type=f32)
                  for hh in heads]
        for hh in heads:
            s = scores[hh] - alibi_ref[hh]
            if bias is not None:
                s = s + bias
            cst = -slopes[hh] * delta_f
            m = ml_ref[2 * hh, 0:1, :]
            l = ml_ref[2 * hh + 1, 0:1, :]
            m_new = jnp.maximum(m, jnp.max(s, axis=0, keepdims=True) + cst)
            alpha = jnp.exp(m - m_new)
            p = jnp.exp(s - (m_new - cst))
            ml_ref[2 * hh, 0:1, :] = m_new
            ml_ref[2 * hh + 1, 0:1, :] = alpha * l + jnp.sum(p, axis=0, keepdims=True)
            acc_ref[hh] = alpha * acc_ref[hh] + jnp.dot(vt, p.astype(mxu_dtype), preferred_element_type=f32)

    def finish(branch):
        for hh in heads:
            out_ref_t[hh] += gates[hh][branch:branch + 1] * (acc_ref[hh] / ml_ref[2 * hh + 1, 0:1, :])

    def sel_rows(c):
        rows = [jnp.broadcast_to(selb_ref[pl.ds(c * blocks_per_chunk + j, 1), :], (SEL_BLOCK, tq))
                for j in range(blocks_per_chunk)]
        return jnp.concatenate(rows, axis=0)

    reset()

    def sel_body(c, carry):
        k0 = pl.multiple_of(c * tk, tk)
        chunk(ksel_ref, vselt_ref, k0, sel_rows(c), (q0 - k0).astype(f32))
        return carry

    lax.fori_loop(0, qi, sel_body, 0)
    k_diag = pl.multiple_of(q0, tk)
    chunk(ksel_ref, vselt_ref, k_diag, sel_rows(qi) + jnp.where(d0 >= 0, 0.0, MASK_VALUE), 0.0)
    finish(1)

    reset()
    for delta in range(0, WINDOW + tk, tk):
        lo_ok = delta - (tk - 1) >= 0
        hi_ok = delta + (tq - 1) < WINDOW
        if lo_ok and hi_ok:
            bias = None
        else:
            dist = d0 + delta
            bias = jnp.where(dist >= 0, jnp.where(dist < WINDOW, 0.0, MASK_VALUE), MASK_VALUE)
        if delta == 0:
            chunk(kwin_ref, vwint_ref, k_diag, bias, 0.0)
        else:
            @pl.when(q0 >= delta)
            def _(delta=delta, bias=bias):
                chunk(kwin_ref, vwint_ref, pl.multiple_of(q0 - delta, tk), bias, float(delta))
    finish(2)

    for hh in heads:
        o_ref[0, :, hh * hd:(hh + 1) * hd] = out_ref_t[hh].T.astype(o_ref.dtype)


def _nsa_stage(big, small, cmp, b, s, tq=256):
    g, hd = NSA_KV_GROUPS, HEAD_DIM
    tq = min(tq, s)
    tk = tq
    n_seg = s // CMP_STRIDE
    n_sel = s // SEL_BLOCK
    big3 = big.reshape(b, s, BIG_WIDTH)
    kv_col = lambda which: (COL_KV + which * NSA_KV_WIDTH) // hd
    vt = lambda which: big3[:, :, COL_KV + which * NSA_KV_WIDTH:COL_KV + (which + 1) * NSA_KV_WIDTH] \
        .reshape(b, s, g, hd).transpose(0, 2, 3, 1).reshape(b * g, hd, s)
    vsel_t, vwin_t = vt(3), vt(5)
    kc = cmp[0]
    vc_t = cmp[1].transpose(0, 2, 1)
    gate_t = small[:, SMALL_GATE:SMALL_GATE + 3 * NSA_HEADS].reshape(b, s, g, 3 * HEADS_PER_GROUP) \
        .transpose(0, 2, 3, 1)
    c_start = np.arange(n_seg)[None, :] * CMP_STRIDE
    s_start = np.arange(n_sel)[:, None] * SEL_BLOCK
    n_cmp = (s - CMP_BLOCK) // CMP_STRIDE + 1
    ovl = ((c_start < s_start + SEL_BLOCK) & (c_start + CMP_BLOCK > s_start)
           & (np.arange(n_seg)[None, :] < n_cmp)).astype(np.float32)
    ovl = jnp.asarray(ovl, jnp.bfloat16)
    slopes = jnp.asarray((2.0 ** (-8.0 * np.arange(1, NSA_HEADS + 1) / NSA_HEADS))
                         .reshape(g, HEADS_PER_GROUP), jnp.float32)
    gw = 3 * HEADS_PER_GROUP
    kernel = functools.partial(_nsa_kernel, tq=tq, tk=tk)
    return pl.pallas_call(
        kernel,
        grid=(b, g, s // tq),
        in_specs=[
            pl.BlockSpec(memory_space=pltpu.SMEM),
            pl.BlockSpec((1, tq, HEADS_PER_GROUP * hd), lambda bi, gi, qi: (bi, qi, gi)),
            pl.BlockSpec((1, n_seg, hd), lambda bi, gi, qi: (bi * NSA_KV_GROUPS + gi, 0, 0)),
            pl.BlockSpec((1, hd, n_seg), lambda bi, gi, qi: (bi * NSA_KV_GROUPS + gi, 0, 0)),
            pl.BlockSpec((1, s, hd), lambda bi, gi, qi: (bi, 0, kv_col(2) + gi)),
            pl.BlockSpec((1, hd, s), lambda bi, gi, qi: (bi * NSA_KV_GROUPS + gi, 0, 0)),
            pl.BlockSpec((1, s, hd), lambda bi, gi, qi: (bi, 0, kv_col(4) + gi)),
            pl.BlockSpec((1, hd, s), lambda bi, gi, qi: (bi * NSA_KV_GROUPS + gi, 0, 0)),
            pl.BlockSpec((1, 1, gw, tq), lambda bi, gi, qi: (bi, gi, 0, qi)),
            pl.BlockSpec((n_sel, n_seg), lambda bi, gi, qi: (0, 0)),
        ],
        out_specs=pl.BlockSpec((1, tq, HEADS_PER_GROUP * hd), lambda bi, gi, qi: (bi, qi, gi)),
        out_shape=jax.ShapeDtypeStruct((b, s, NSA_Q_WIDTH), big.dtype),
        scratch_shapes=[
            pltpu.VMEM((n_sel, tq), jnp.float32),
            pltpu.VMEM((HEADS_PER_GROUP, tk, tq), jnp.float32),
            pltpu.VMEM((HEADS_PER_GROUP, hd, tq), jnp.float32),
            pltpu.VMEM((2 * HEADS_PER_GROUP, 8, tq), jnp.float32),
            pltpu.VMEM((HEADS_PER_GROUP, hd, tq), jnp.float32),
        ],
        compiler_params=_params(("parallel", "parallel", "arbitrary")),
    )(slopes, big3, kc, vc_t, big3, vsel_t, big3, vwin_t, gate_t, ovl)


_TN = (((0,), (0,)), ((), ()))
_HI = lax.Precision.HIGHEST


def _gla_kernel(q_ref, k_ref, v_ref, og_ref, sm_ref, wa_ref, ba_ref, gn_ref, selw_ref, o_ref,
                state_ref, *, n_chunks):
    f32 = jnp.float32
    mxu_dtype = v_ref.dtype
    c_len, sub = GLA_CHUNK, GLA_SUB
    ns = c_len // sub
    dk = GLA_KEY_DIM

    @pl.when(pl.program_id(2) == 0)
    def _():
        state_ref[...] = jnp.zeros_like(state_ref)

    tc = n_chunks * c_len
    chunks = range(n_chunks)
    row = lax.broadcasted_iota(jnp.int32, (c_len, 1), 0)
    col = lax.broadcasted_iota(jnp.int32, (1, c_len), 1)
    sub_row = row // sub
    sub_col = col // sub
    t_loc = lax.broadcasted_iota(jnp.int32, (1, sub, 1), 1)
    neg_inf = -jnp.inf

    logits = jnp.dot(sm_ref[0], wa_ref[0], precision=_HI, preferred_element_type=f32) + ba_ref[0]
    glog = (jnp.minimum(logits, 0.0) - jnp.log(1.0 + jnp.exp(-jnp.abs(logits)))) * (1.0 / GATE_TAU)
    row_b = lax.broadcasted_iota(jnp.int32, (tc, 1), 0)
    col_b = lax.broadcasted_iota(jnp.int32, (1, tc), 1)
    block_tril = jnp.where(col_b <= row_b, (row_b // c_len == col_b // c_len).astype(f32), 0.0)
    block_tril = block_tril.astype(jnp.bfloat16)
    bcum_all = jnp.zeros((tc, dk), f32)
    for piece in _split3(glog):
        bcum_all = bcum_all + jnp.dot(block_tril, piece, preferred_element_type=f32)

    rows = [slice(c * c_len, (c + 1) * c_len) for c in chunks]
    qs = [q_ref[0, rows[c], :].astype(f32) for c in chunks]
    ks = [k_ref[0, rows[c], :].astype(f32) for c in chunks]
    vs = [v_ref[0, rows[c], :] for c in chunks]
    bcums = [bcum_all[rows[c], :] for c in chunks]
    b_lasts = [b[c_len - 1:c_len, :] for b in bcums]

    q_ins, kfs, qfs, ecats, kds = [], [], [], [], []
    for c in chunks:
        q, k, bcum = qs[c], ks[c], bcums[c]
        q_ins.append((q * jnp.exp(bcum)).astype(mxu_dtype))
        r = [bcum[(j + 1) * sub - 1:(j + 1) * sub, :] for j in range(ns)]
        r_rows = jnp.concatenate([jnp.broadcast_to(rj, (sub, dk)) for rj in r], axis=0)
        kfs.append((k * jnp.exp(r_rows - bcum)).astype(mxu_dtype))
        qfs.append([(q * jnp.exp(jnp.where(row >= (j + 1) * sub, bcum - r[j], neg_inf))).astype(mxu_dtype)
                    for j in range(ns - 1)])
        q3 = q.reshape(ns, sub, dk)
        k3 = k.reshape(ns, sub, dk)
        b3 = bcum.reshape(ns, sub, dk)
        pieces = []
        for s_ in range(sub):
            arg = jnp.where(t_loc >= s_, b3 - b3[:, s_:s_ + 1, :], neg_inf)
            e = q3 * k3[:, s_:s_ + 1, :] * jnp.exp(arg)
            pieces.append(e.reshape(c_len, dk).astype(mxu_dtype))
        ecats.append(jnp.concatenate(pieces, axis=1))
        kds.append((k * jnp.exp(b_lasts[c] - bcum)).astype(mxu_dtype))

    ajs = [[lax.dot_general(qfs[c][j], kfs[c], _NT, preferred_element_type=f32) for j in range(ns - 1)]
           for c in chunks]
    d_wides = [jnp.dot(ecats[c], selw_ref[...], preferred_element_type=f32) for c in chunks]
    updates = [lax.dot_general(vs[c], kds[c], _TN, preferred_element_type=f32) for c in chunks]
    o_intras = []
    for c in chunks:
        a = jnp.where(sub_row == sub_col, d_wides[c], 0.0)
        for j in range(ns - 1):
            a = a + jnp.where(sub_col == j, ajs[c][j], 0.0)
        o_intras.append(jnp.dot(a.astype(mxu_dtype), vs[c], preferred_element_type=f32))

    st = state_ref[...]
    for c in chunks:
        o = o_intras[c] + lax.dot_general(q_ins[c], st.astype(mxu_dtype), _NT, preferred_element_type=f32)
        st = st * jnp.exp(b_lasts[c]) + updates[c]
        rms = lax.rsqrt(jnp.mean(o * o, axis=-1, keepdims=True) + EPS)
        gate = og_ref[0, rows[c], :].astype(f32)
        y = o * rms * gn_ref[...] * (gate * jax.nn.sigmoid(gate))
        o_ref[0, rows[c], :] = y.astype(o_ref.dtype)
    state_ref[...] = st


def _gla_stage(big, small, w_alpha2, b_alpha, g_norm, b, s, tc=512):
    h, dk, dv = GLA_HEADS, GLA_KEY_DIM, GLA_VAL_DIM
    tc = min(tc, s)
    big3 = big.reshape(b, s, BIG_WIDTH)
    small3 = small.reshape(b, s, SMALL_WIDTH)
    wa = jnp.zeros((h, SMALL_WIDTH, dk), jnp.float32).at[:, SMALL_ALPHA:SMALL_ALPHA + GATE_RANK, :].set(
        w_alpha2.astype(jnp.float32).reshape(GATE_RANK, h, dk).transpose(1, 0, 2))
    ba = b_alpha.astype(jnp.float32).reshape(h, 1, dk)
    gn = g_norm.astype(jnp.float32).reshape(1, dv)
    sel = (np.arange(GLA_SUB * dk)[:, None] // dk == np.arange(GLA_CHUNK)[None, :] % GLA_SUB)
    selw = jnp.asarray(sel.astype(np.float32), big.dtype)
    kernel = functools.partial(_gla_kernel, n_chunks=tc // GLA_CHUNK)
    return pl.pallas_call(
        kernel,
        grid=(b, h, s // tc),
        in_specs=[
            pl.BlockSpec((1, tc, dk), lambda bi, hi, ci: (bi, ci, COL_GLA_Q // dk + hi)),
            pl.BlockSpec((1, tc, dk), lambda bi, hi, ci: (bi, ci, COL_GLA_K // dk + hi)),
            pl.BlockSpec((1, tc, dv), lambda bi, hi, ci: (bi, ci, COL_GLA_V // dv + hi)),
            pl.BlockSpec((1, tc, dv), lambda bi, hi, ci: (bi, ci, COL_GLA_GATE // dv + hi)),
            pl.BlockSpec((1, tc, SMALL_WIDTH), lambda bi, hi, ci: (bi, ci, 0)),
            pl.BlockSpec((1, SMALL_WIDTH, dk), lambda bi, hi, ci: (hi, 0, 0)),
            pl.BlockSpec((1, 1, dk), lambda bi, hi, ci: (hi, 0, 0)),
            pl.BlockSpec((1, dv), lambda bi, hi, ci: (0, 0)),
            pl.BlockSpec((GLA_SUB * dk, GLA_CHUNK), lambda bi, hi, ci: (0, 0)),
        ],
        out_specs=pl.BlockSpec((1, tc, dv), lambda bi, hi, ci: (bi, ci, hi)),
        out_shape=jax.ShapeDtypeStruct((b, s, GLA_V_WIDTH), big.dtype),
        scratch_shapes=[pltpu.VMEM((dv, dk), jnp.float32)],
        compiler_params=_params(("parallel", "parallel", "arbitrary")),
    )(big3, big3, big3, big3, small3, wa, ba, gn, selw)


ROUTE_E0 = N_EXPERT_GROUPS
ROUTE_ID = ROUTE_E0 + N_EXPERTS
ROUTE_W = ROUTE_ID + 2
MOE_TM = 256
GATHER_UNROLL = 8


def _out_proj_kernel(x_ref, nsa_ref, gla_ref, wt_ref, wb_ref, g_ref, wr_ref, br_ref,
                     x1_ref, h_ref, comb_ref):
    f32 = jnp.float32
    acc = jnp.dot(nsa_ref[...], wt_ref[...], preferred_element_type=f32)
    acc = acc + jnp.dot(gla_ref[...], wb_ref[...], preferred_element_type=f32)
    x1 = x_ref[...] + acc
    x1_ref[...] = x1
    hf = x1 * lax.rsqrt(jnp.mean(x1 * x1, axis=-1, keepdims=True) + EPS) * g_ref[...]
    half = hf.shape[1] // 2
    bits = pltpu.bitcast(hf.astype(jnp.bfloat16).astype(f32), jnp.uint32)
    h_ref[...] = lax.shift_right_logical(bits[:, :half], jnp.uint32(16)) | (bits[:, half:] & jnp.uint32(0xFFFF0000))

    h_hi, h_mid, _ = _split3(hf)
    logits = (jnp.dot(h_hi, wr_ref[0], preferred_element_type=f32)
              + jnp.dot(h_hi, wr_ref[1], preferred_element_type=f32)
              + jnp.dot(h_mid, wr_ref[0], preferred_element_type=f32)) + br_ref[...]

    lane = lax.broadcasted_iota(jnp.int32, (1, LANES), 1).astype(f32)
    big_lane = float(LANES)
    neg_inf = -jnp.inf
    first_argmax = lambda vals, vmax: jnp.min(jnp.where(vals == vmax, lane, big_lane), axis=-1, keepdims=True)
    gl = jnp.where(lane < N_EXPERT_GROUPS, logits, neg_inf)
    gmax = jnp.max(gl, axis=-1, keepdims=True)
    gsel = first_argmax(gl, gmax)
    gw = 1.0 / jnp.sum(jnp.exp(gl - gmax), axis=-1, keepdims=True)
    lo = ROUTE_E0 + EXPERTS_PER_GROUP * gsel
    el = jnp.where(lane >= lo, jnp.where(lane < lo + EXPERTS_PER_GROUP, logits, neg_inf), neg_inf)
    m1 = jnp.max(el, axis=-1, keepdims=True)
    i1 = first_argmax(el, m1)
    el2 = jnp.where(lane == i1, neg_inf, el)
    m2 = jnp.max(el2, axis=-1, keepdims=True)
    i2 = first_argmax(el2, m2)
    e2 = jnp.exp(m2 - m1)
    w1 = gw / (1.0 + e2)
    w2 = w1 * e2
    onehot = jnp.where(lane == i1, 1.0, 0.0) + jnp.where(lane == i2, 1.0, 0.0)
    meta = (jnp.where(lane == ROUTE_ID, i1 - ROUTE_E0, 0.0) + jnp.where(lane == ROUTE_ID + 1, i2 - ROUTE_E0, 0.0)
            + jnp.where(lane == ROUTE_W, w1, 0.0) + jnp.where(lane == ROUTE_W + 1, w2, 0.0))
    comb_ref[...] = onehot + meta


def _out_proj_stage(x2, nsa, gla, w_out, g_ffn, w_rg, b_rg, w_re, b_re, tm=512):
    t, d = x2.shape
    tm = min(tm, t)
    half = nsa.shape[1]
    w_top = w_out[:half].astype(nsa.dtype)
    w_bot = w_out[half:].astype(nsa.dtype)
    wr = jnp.concatenate([w_rg, w_re.reshape(d, N_EXPERTS),
                          jnp.zeros((d, LANES - ROUTE_E0 - N_EXPERTS), jnp.float32)], axis=1)
    wr_hi, wr_mid, _ = _split3(wr)
    wr2 = jnp.stack([wr_hi, wr_mid])
    br = jnp.concatenate([b_rg, b_re.reshape(N_EXPERTS),
                          jnp.zeros((LANES - ROUTE_E0 - N_EXPERTS,), jnp.float32)])[None]
    row = lambda m: (m, 0)
    fixed = lambda m: (0, 0)
    return pl.pallas_call(
        _out_proj_kernel,
        grid=(t // tm,),
        in_specs=[
            pl.BlockSpec((tm, d), row),
            pl.BlockSpec((tm, half), row),
            pl.BlockSpec((tm, half), row),
            pl.BlockSpec((half, d), fixed),
            pl.BlockSpec((half, d), fixed),
            pl.BlockSpec((1, d), fixed),
            pl.BlockSpec((2, d, LANES), lambda m: (0, 0, 0)),
            pl.BlockSpec((1, LANES), fixed),
        ],
        out_specs=[pl.BlockSpec((tm, d), row), pl.BlockSpec((tm, d // 2), row), pl.BlockSpec((tm, LANES), row)],
        out_shape=[
            jax.ShapeDtypeStruct((t, d), jnp.float32),
            jax.ShapeDtypeStruct((t, d // 2), jnp.uint32),
            jax.ShapeDtypeStruct((t, LANES), jnp.float32),
        ],
        compiler_params=_params(("parallel",)),
    )(x2, nsa, gla, w_top, w_bot, g_ffn.astype(jnp.float32)[None], wr2, br)


def _rank_kernel(route_ref, rank_ref, count_ref, carry_ref):
    f32 = jnp.float32
    tm = route_ref.shape[0]

    @pl.when(pl.program_id(0) == 0)
    def _():
        carry_ref[...] = jnp.zeros_like(carry_ref)

    route = route_ref[...]
    lane = lax.broadcasted_iota(jnp.int32, (1, LANES), 1)
    lane_f = lane.astype(f32)
    onehot = jnp.where((lane >= ROUTE_E0) & (lane < ROUTE_ID), route, 0.0)
    row = lax.broadcasted_iota(jnp.int32, (tm, tm), 0)
    col = lax.broadcasted_iota(jnp.int32, (tm, tm), 1)
    strict_lower = (col < row).astype(jnp.bfloat16)
    before = jnp.dot(strict_lower, onehot.astype(jnp.bfloat16), preferred_element_type=f32) + carry_ref[0:1, :]
    ranks = []
    for j in range(2):
        lane_j = route[:, ROUTE_ID + j:ROUTE_ID + j + 1] + float(ROUTE_E0)
        ranks.append(jnp.sum(jnp.where(lane_f == lane_j, before, 0.0), axis=-1, keepdims=True))
    rank_ref[...] = jnp.where(lane == 0, ranks[0], jnp.where(lane == 1, ranks[1], 0.0))
    total = carry_ref[0:1, :] + jnp.sum(onehot, axis=0, keepdims=True)
    carry_ref[...] = jnp.broadcast_to(total, carry_ref.shape)
    count_ref[...] = jnp.broadcast_to(total, count_ref.shape)


def _rank_stage(route, tm=512):
    t = route.shape[0]
    tm = min(tm, t)
    return pl.pallas_call(
        _rank_kernel,
        grid=(t // tm,),
        in_specs=[pl.BlockSpec((tm, LANES), lambda m: (m, 0))],
        out_specs=[pl.BlockSpec((tm, LANES), lambda m: (m, 0)), pl.BlockSpec((8, LANES), lambda m: (0, 0))],
        out_shape=[jax.ShapeDtypeStruct((t, LANES), jnp.float32), jax.ShapeDtypeStruct((8, LANES), jnp.float32)],
        scratch_shapes=[pltpu.VMEM((8, LANES), jnp.float32)],
        compiler_params=_params(("arbitrary",)),
    )(route)


def _gather_rows(idx_ref, n_rows, src_hbm, dst, sem):
    def body(r, carry):
        tok = idx_ref[0, 0, r]
        pltpu.make_async_copy(src_hbm.at[pl.ds(tok, 1), :], dst.at[pl.ds(r, 1), :], sem).start()
        return carry
    lax.fori_loop(0, n_rows, body, 0, unroll=GATHER_UNROLL)


def _wait_rows(n_rows, src_hbm, dst, sem):
    pltpu.make_async_copy(src_hbm.at[pl.ds(0, n_rows), :], dst, sem).wait()


def _moe_kernel(te_ref, na_ref, idx_ref, idx_next_ref, h_hbm, wg_ref, wu_ref, wd_ref, y_ref,
                buf, sem, wg_bf, wu_bf, wd_bf):
    f32 = jnp.float32
    bf16 = jnp.bfloat16
    i = pl.program_id(0)
    n_active = na_ref[0]
    tm = buf.shape[1]
    slot = i % 2

    @pl.when(i == 0)
    def _():
        _gather_rows(idx_ref, tm, h_hbm, buf.at[0], sem.at[0])

    @pl.when(i < n_active)
    def _():
        expert_changed = (i == 0) | (te_ref[i] != te_ref[jnp.maximum(i - 1, 0)])

        @pl.when(expert_changed)
        def _():
            wg_bf[...] = wg_ref[0].astype(bf16)
            wu_bf[...] = wu_ref[0].astype(bf16)
            wd_bf[...] = wd_ref[0].astype(bf16)

        for cur in range(2):
            @pl.when(slot == cur)
            def _(cur=cur):
                nxt = 1 - cur
                _wait_rows(tm, h_hbm, buf.at[cur], sem.at[cur])
                for r in range(tm):
                    tok = idx_next_ref[0, 0, r]
                    pltpu.make_async_copy(h_hbm.at[pl.ds(tok, 1), :], buf.at[nxt, pl.ds(r, 1), :],
                                          sem.at[nxt]).start()
                words = buf[cur]
                half = words.shape[1]
                x_lo = pltpu.bitcast(lax.shift_left(words, jnp.uint32(16)), f32).astype(bf16)
                x_hi = pltpu.bitcast(words & jnp.uint32(0xFFFF0000), f32).astype(bf16)
                gate = (jnp.dot(x_lo, wg_bf[:half, :], preferred_element_type=f32)
                        + jnp.dot(x_hi, wg_bf[half:, :], preferred_element_type=f32))
                up = (jnp.dot(x_lo, wu_bf[:half, :], preferred_element_type=f32)
                      + jnp.dot(x_hi, wu_bf[half:, :], preferred_element_type=f32))
                hid = (gate * jax.nn.sigmoid(gate) * up).astype(bf16)
                y_ref[...] = jnp.dot(hid, wd_bf[...], preferred_element_type=f32)

    @pl.when(i == n_active)
    def _():
        _wait_rows(tm, h_hbm, buf.at[slot], sem.at[slot])

    @pl.when(i >= n_active)
    def _():
        y_ref[...] = jnp.zeros_like(y_ref)


def _moe_stage(h2, tok_of_pos, tile_expert, n_active, w_gate, w_up, w_down):
    half = h2.shape[1]
    d = 2 * half
    n_tiles, _, tm = tok_of_pos.shape
    f = w_gate.shape[-1]
    wg = w_gate.reshape(N_EXPERTS, d, f)
    wu = w_up.reshape(N_EXPERTS, d, f)
    wd = w_down.reshape(N_EXPERTS, f, d)
    grid_spec = pltpu.PrefetchScalarGridSpec(
        num_scalar_prefetch=2,
        grid=(n_tiles,),
        in_specs=[
            pl.BlockSpec((1, 1, tm), lambda i, te, na: (i, 0, 0), memory_space=pltpu.SMEM),
            pl.BlockSpec((1, 1, tm), lambda i, te, na: (jnp.minimum(i + 1, n_tiles - 1), 0, 0),
                         memory_space=pltpu.SMEM),
            pl.BlockSpec(memory_space=pl.ANY),
            pl.BlockSpec((1, d, f), lambda i, te, na: (te[i], 0, 0)),
            pl.BlockSpec((1, d, f), lambda i, te, na: (te[i], 0, 0)),
            pl.BlockSpec((1, f, d), lambda i, te, na: (te[i], 0, 0)),
        ],
        out_specs=pl.BlockSpec((tm, d), lambda i, te, na: (i, 0)),
        scratch_shapes=[
            pltpu.VMEM((2, tm, half), jnp.uint32), pltpu.SemaphoreType.DMA((2,)),
            pltpu.VMEM((d, f), jnp.bfloat16), pltpu.VMEM((d, f), jnp.bfloat16), pltpu.VMEM((f, d), jnp.bfloat16),
        ],
    )
    return pl.pallas_call(
        _moe_kernel,
        grid_spec=grid_spec,
        out_shape=jax.ShapeDtypeStruct((n_tiles * tm, d), jnp.float32),
        compiler_params=_params(("arbitrary",)),
    )(tile_expert, n_active, tok_of_pos, tok_of_pos, h2, wg, wu, wd)


def _combine_kernel(p1_ref, p2_ref, p1n_ref, p2n_ref, route_ref, x1_ref, gf_ref, y_hbm, o_ref, buf, sem,
                    *, final_norm):
    i = pl.program_id(0)
    n = pl.num_programs(0)
    tm = buf.shape[2]

    def gather(a_ref, b_ref, slot):
        for j, idx_ref in enumerate((a_ref, b_ref)):
            for r in range(tm):
                pltpu.make_async_copy(y_hbm.at[pl.ds(idx_ref[0, 0, r], 1), :],
                                      buf.at[slot, j, pl.ds(r, 1), :], sem.at[slot, j]).start()

    @pl.when(i == 0)
    def _():
        gather(p1_ref, p2_ref, 0)

    @pl.when(i + 1 < n)
    def _():
        gather(p1n_ref, p2n_ref, (i + 1) % 2)

    slot = i % 2
    _wait_rows(tm, y_hbm, buf.at[slot, 0], sem.at[slot, 0])
    _wait_rows(tm, y_hbm, buf.at[slot, 1], sem.at[slot, 1])
    route = route_ref[...]
    w1 = route[:, ROUTE_W:ROUTE_W + 1]
    w2 = route[:, ROUTE_W + 1:ROUTE_W + 2]
    y = x1_ref[...] + (w1 * buf[slot, 0] + w2 * buf[slot, 1])
    if final_norm:
        y = y * lax.rsqrt(jnp.mean(y * y, axis=-1, keepdims=True) + EPS) * gf_ref[...]
    o_ref[...] = y


def _combine_stage(y_sorted, pos1, pos2, route, x1, g_final, final_norm, tm=256):
    t, d = x1.shape
    tm = min(tm, t)
    n = t // tm
    p1 = pos1.reshape(n, 1, tm)
    p2 = pos2.reshape(n, 1, tm)
    cur = lambda i: (i, 0, 0)
    nxt = lambda i: (jnp.minimum(i + 1, n - 1), 0, 0)
    smem = lambda index_map: pl.BlockSpec((1, 1, tm), index_map, memory_space=pltpu.SMEM)
    return pl.pallas_call(
        functools.partial(_combine_kernel, final_norm=final_norm),
        grid=(n,),
        in_specs=[
            smem(cur), smem(cur), smem(nxt), smem(nxt),
            pl.BlockSpec((tm, LANES), lambda i: (i, 0)),
            pl.BlockSpec((tm, d), lambda i: (i, 0)),
            pl.BlockSpec((1, d), lambda i: (0, 0)),
            pl.BlockSpec(memory_space=pl.ANY),
        ],
        out_specs=pl.BlockSpec((tm, d), lambda i: (i, 0)),
        out_shape=jax.ShapeDtypeStruct((t, d), jnp.float32),
        scratch_shapes=[pltpu.VMEM((2, 2, tm, d), jnp.float32), pltpu.SemaphoreType.DMA((2, 2))],
        compiler_params=_params(("arbitrary",)),
    )(p1, p2, p1, p2, route, x1, g_final.astype(jnp.float32)[None], y_sorted)


def _routing_tables(route, rank, count, tm):
    t = route.shape[0]
    e1 = route[:, ROUTE_ID].astype(jnp.int32)
    e2 = route[:, ROUTE_ID + 1].astype(jnp.int32)
    counts = count[0, ROUTE_E0:ROUTE_E0 + N_EXPERTS].astype(jnp.int32)
    padded = ((counts + tm - 1) // tm) * tm
    ends = jnp.cumsum(padded)
    offs = ends - padded
    pos1 = offs[e1] + rank[:, 0].astype(jnp.int32)
    pos2 = offs[e2] + rank[:, 1].astype(jnp.int32)
    n_tiles = (2 * t) // tm + N_EXPERTS
    tok = jnp.arange(t, dtype=jnp.int32)
    tok_of_pos = jnp.zeros((n_tiles * tm,), jnp.int32).at[jnp.concatenate([pos1, pos2])].set(
        jnp.concatenate([tok, tok]), unique_indices=True)
    tile_start = jnp.arange(n_tiles, dtype=jnp.int32) * tm
    tile_expert = jnp.minimum(jnp.sum((ends[None, :] <= tile_start[:, None]).astype(jnp.int32), axis=1),
                              N_EXPERTS - 1)
    n_active = (ends[-1] // tm).astype(jnp.int32).reshape(1)
    return pos1, pos2, tok_of_pos.reshape(n_tiles, 1, tm), tile_expert, n_active


def kernel(x, g_mix_norm, w_in, b_nsa_gate, cmp_pos_k, w_cmp_k1, w_cmp_k2, cmp_pos_v, w_cmp_v1, w_cmp_v2,
           w_alpha2, b_alpha, g_gla_norm, w_out, g_ffn_norm, w_router_group, b_router_group,
           w_router_expert, b_router_expert, w_expert_gate, w_expert_up, w_expert_down, g_final_norm):
    b, s, d = x.shape
    depth = w_in.shape[0]
    x2 = x.reshape(b * s, d)
    for l in range(depth):
        last = l == depth - 1
        w_big, w_small, col_scale, small_bias = _prep_in_proj_weights(w_in[l], b_nsa_gate[l])
        big, small = _in_proj(x2, g_mix_norm[l].astype(jnp.float32)[None], w_big, w_small, col_scale, small_bias)
        cmp = _compress_stage(big, b, s, cmp_pos_k[l], w_cmp_k1[l], w_cmp_k2[l],
                              cmp_pos_v[l], w_cmp_v1[l], w_cmp_v2[l])
        nsa = _nsa_stage(big, small, cmp, b, s).reshape(b * s, NSA_Q_WIDTH)
        gla = _gla_stage(big, small, w_alpha2[l], b_alpha[l], g_gla_norm[l], b, s).reshape(b * s, GLA_V_WIDTH)
        x1, h2, route = _out_proj_stage(x2, nsa, gla, w_out[l], g_ffn_norm[l], w_router_group[l],
                                        b_router_group[l], w_router_expert[l], b_router_expert[l])
        rank, count = _rank_stage(route)
        pos1, pos2, tok_of_pos, tile_expert, n_active = _routing_tables(route, rank, count, MOE_TM)
        y_sorted = _moe_stage(h2, tok_of_pos, tile_expert, n_active,
                              w_expert_gate[l], w_expert_up[l], w_expert_down[l])
        x2 = _combine_stage(y_sorted, pos1, pos2, route, x1, g_final_norm, last)
    return x2.reshape(b, s, d)
```

```python
import functools

import numpy as np
import jax
import jax.numpy as jnp
from jax import lax
from jax.experimental import pallas as pl
from jax.experimental.pallas import tpu as pltpu

NSA_HEADS = 8
NSA_KV_GROUPS = 2
HEADS_PER_GROUP = NSA_HEADS // NSA_KV_GROUPS
HEAD_DIM = 128
CMP_BLOCK = 32
CMP_STRIDE = 16
SEL_BLOCK = 64
SEL_TOPK = 8
N_LOCAL_BLOCKS = 2
WINDOW = 512
GLA_HEADS = 4
GLA_KEY_DIM = 128
GLA_VAL_DIM = 256
GATE_RANK = 16
GATE_TAU = 16.0
GLA_CHUNK = 64
GLA_SUB = 16
N_EXPERT_GROUPS = 4
EXPERTS_PER_GROUP = 8
N_EXPERTS = N_EXPERT_GROUPS * EXPERTS_PER_GROUP
EPS = 1e-6
MASK_VALUE = -1e30

NSA_Q_WIDTH = NSA_HEADS * HEAD_DIM
NSA_KV_WIDTH = NSA_KV_GROUPS * HEAD_DIM
GLA_QK_WIDTH = GLA_HEADS * GLA_KEY_DIM
GLA_V_WIDTH = GLA_HEADS * GLA_VAL_DIM

LANES = 128
VMEM_LIMIT_BYTES = 56 * 1024 * 1024

COL_NSA_Q = 0
COL_KV = COL_NSA_Q + NSA_Q_WIDTH
COL_GLA_Q = COL_KV + 6 * NSA_KV_WIDTH
COL_GLA_K = COL_GLA_Q + GLA_QK_WIDTH
COL_GLA_V = COL_GLA_K + GLA_QK_WIDTH
COL_GLA_GATE = COL_GLA_V + GLA_V_WIDTH
BIG_WIDTH = COL_GLA_GATE + GLA_V_WIDTH
SMALL_GATE = 0
SMALL_ALPHA = 3 * NSA_HEADS
SMALL_WIDTH = LANES


def _params(semantics):
    return pltpu.CompilerParams(dimension_semantics=semantics, vmem_limit_bytes=VMEM_LIMIT_BYTES)


def _split3(a):
    hi = a.astype(jnp.bfloat16)
    r1 = a - hi.astype(jnp.float32)
    mid = r1.astype(jnp.bfloat16)
    lo = (r1 - mid.astype(jnp.float32)).astype(jnp.bfloat16)
    return hi, mid, lo


def _in_proj_kernel(x_ref, g_ref, wb_ref, ws_ref, scale_ref, sbias_ref, big_ref, small_ref, h_ref):
    n = pl.program_id(1)

    @pl.when(n == 0)
    def _():
        xf = x_ref[...]
        y = xf * lax.rsqrt(jnp.mean(xf * xf, axis=-1, keepdims=True) + EPS)
        h = (y * g_ref[...]).astype(jnp.bfloat16)
        h_ref[...] = h
        small_ref[...] = jnp.dot(h, ws_ref[...], preferred_element_type=jnp.float32) + sbias_ref[...]

    acc = jnp.dot(h_ref[...], wb_ref[...], preferred_element_type=jnp.float32)
    big_ref[...] = (acc * scale_ref[...]).astype(big_ref.dtype)


def _in_proj(x2, gain, w_big, w_small, col_scale, small_bias, tm=1024, tn=1408):
    t, d = x2.shape
    tm = min(tm, t)
    grid = (t // tm, BIG_WIDTH // tn)
    return pl.pallas_call(
        _in_proj_kernel,
        grid=grid,
        in_specs=[
            pl.BlockSpec((tm, d), lambda m, n: (m, 0)),
            pl.BlockSpec((1, d), lambda m, n: (0, 0)),
            pl.BlockSpec((d, tn), lambda m, n: (0, n)),
            pl.BlockSpec((d, SMALL_WIDTH), lambda m, n: (0, 0)),
            pl.BlockSpec((1, tn), lambda m, n: (0, n)),
            pl.BlockSpec((1, SMALL_WIDTH), lambda m, n: (0, 0)),
        ],
        out_specs=[
            pl.BlockSpec((tm, tn), lambda m, n: (m, n)),
            pl.BlockSpec((tm, SMALL_WIDTH), lambda m, n: (m, 0)),
        ],
        out_shape=[
            jax.ShapeDtypeStruct((t, BIG_WIDTH), jnp.bfloat16),
            jax.ShapeDtypeStruct((t, SMALL_WIDTH), jnp.float32),
        ],
        scratch_shapes=[pltpu.VMEM((tm, d), jnp.bfloat16)],
        compiler_params=_params(("parallel", "arbitrary")),
    )(x2, gain, w_big, w_small, col_scale, small_bias)


def _prep_in_proj_weights(w_in, b_nsa_gate):
    sizes = (NSA_Q_WIDTH,) + (NSA_KV_WIDTH,) * 6 + (3 * NSA_HEADS, GLA_QK_WIDTH, GLA_QK_WIDTH,
                                                    GLA_V_WIDTH, GATE_RANK, GLA_V_WIDTH)
    offs = np.concatenate([[0], np.cumsum(sizes)])
    part = lambda i: w_in[:, offs[i]:offs[i + 1]]
    w_big = jnp.concatenate([part(i) for i in (0, 1, 2, 3, 4, 5, 6, 8, 9, 10, 12)], axis=1)
    pad = jnp.zeros((w_in.shape[0], SMALL_WIDTH - 3 * NSA_HEADS - GATE_RANK), w_in.dtype)
    w_small = jnp.concatenate([part(7), part(11), pad], axis=1)
    scale = np.ones((1, BIG_WIDTH), np.float32)
    scale[0, COL_NSA_Q:COL_NSA_Q + NSA_Q_WIDTH] = HEAD_DIM ** -0.5
    scale[0, COL_GLA_Q:COL_GLA_Q + GLA_QK_WIDTH] = GLA_KEY_DIM ** -0.5
    small_bias = jnp.concatenate([b_nsa_gate.astype(jnp.float32),
                                  jnp.zeros((SMALL_WIDTH - 3 * NSA_HEADS,), jnp.float32)])[None]
    return w_big.astype(jnp.bfloat16), w_small.astype(jnp.bfloat16), jnp.asarray(scale), small_bias


def _gelu_tanh(a):
    return 0.5 * a * (1.0 + jnp.tanh(np.sqrt(2.0 / np.pi) * (a + 0.044715 * (a * a * a))))


def _compress_kernel(z_ref, pos_ref, w1_ref, w2_ref, o_ref):
    z = z_ref[0].astype(jnp.float32)
    pos = pos_ref[0]
    z_lo = (z + pos[0:1]).astype(jnp.bfloat16)
    z_hi = (z + pos[1:2]).astype(jnp.bfloat16)
    u = jnp.dot(z_lo, w1_ref[0, 0], preferred_element_type=jnp.float32)
    v = jnp.dot(z_hi, w1_ref[0, 1], preferred_element_type=jnp.float32)
    a = u + pltpu.roll(v, v.shape[0] - 1, axis=0)
    hid = _gelu_tanh(a).astype(jnp.bfloat16)
    o_ref[0] = jnp.dot(hid, w2_ref[0], preferred_element_type=jnp.float32).astype(o_ref.dtype)


def _compress(z, pos, w1, w2, tr):
    _, rows, width = z.shape
    hid = w1.shape[-1]
    return pl.pallas_call(
        _compress_kernel,
        grid=(2, rows // tr),
        in_specs=[
            pl.BlockSpec((1, tr, width), lambda c, r: (c, r, 0)),
            pl.BlockSpec((1, 2, width), lambda c, r: (c, 0, 0)),
            pl.BlockSpec((1, 2, width, hid), lambda c, r: (c, 0, 0, 0)),
            pl.BlockSpec((1, hid, HEAD_DIM), lambda c, r: (c, 0, 0)),
        ],
        out_specs=pl.BlockSpec((1, tr, HEAD_DIM), lambda c, r: (c, r, 0)),
        out_shape=jax.ShapeDtypeStruct((2, rows, HEAD_DIM), jnp.bfloat16),
        compiler_params=_params(("parallel", "parallel")),
    )(z, pos, w1, w2)


def _compress_stage(big, b, s, cmp_pos_k, w_cmp_k1, w_cmp_k2, cmp_pos_v, w_cmp_v1, w_cmp_v2):
    g, dk = NSA_KV_GROUPS, HEAD_DIM
    n_seg = s // CMP_STRIDE
    kv = big[:, COL_KV:COL_KV + 2 * NSA_KV_WIDTH].reshape(b, n_seg, CMP_STRIDE, 2, g, dk)
    z = kv.transpose(3, 0, 4, 1, 2, 5).reshape(2, b * g * n_seg, CMP_STRIDE * dk)
    pos = jnp.stack([cmp_pos_k, cmp_pos_v]).reshape(2, 2, CMP_STRIDE * dk)
    w1 = jnp.stack([w_cmp_k1, w_cmp_v1]).reshape(2, 2, CMP_STRIDE * dk, -1).astype(jnp.bfloat16)
    w2 = jnp.stack([w_cmp_k2, w_cmp_v2]).astype(jnp.bfloat16)
    pairs = b * g
    per_tile = max(p for p in (1, 2, 4, 8) if pairs % p == 0 and p * n_seg <= 1024)
    out = _compress(z, pos, w1, w2, per_tile * n_seg)
    return out.reshape(2, b * g, n_seg, dk)


_NT = (((1,), (1,)), ((), ()))


def _nsa_kernel(slopes_ref, q_ref, kc_ref, vct_ref, ksel_ref, vselt_ref, kwin_ref, vwint_ref,
                gate_ref, ovl_ref, o_ref, selb_ref, alibi_ref, acc_ref, ml_ref, out_ref_t, *, tq, tk):
    g = pl.program_id(1)
    qi = pl.program_id(2)
    q0 = qi * tq
    n_seg = kc_ref.shape[1]
    n_sel = ovl_ref.shape[0]
    mxu_dtype = ksel_ref.dtype
    f32 = jnp.float32
    hd = HEAD_DIM

    t_row = q0 + lax.broadcasted_iota(jnp.int32, (1, tq), 1)

    cmp_end = lax.broadcasted_iota(jnp.int32, (n_seg, 1), 0) * CMP_STRIDE + (CMP_BLOCK - 1)
    valid_c = cmp_end <= t_row
    dist_c = (t_row - cmp_end).astype(f32)
    any_c = (t_row >= CMP_BLOCK - 1).astype(f32)
    kc = kc_ref[0]
    vct = vct_ref[0]
    psum = jnp.zeros((n_seg, tq), f32)
    heads = range(HEADS_PER_GROUP)
    slopes = [slopes_ref[g, hh] for hh in heads]
    gates = [jax.nn.sigmoid(gate_ref[0, 0, 3 * hh:3 * hh + 3, :]) for hh in heads]
    for hh in heads:
        qh = q_ref[0, :, hh * hd:(hh + 1) * hd]
        sc = lax.dot_general(kc, qh, _NT, preferred_element_type=f32)
        sc = jnp.where(valid_c, sc - slopes[hh] * dist_c, MASK_VALUE)
        m = jnp.max(sc, axis=0, keepdims=True)
        p = jnp.exp(sc - m)
        p = p * (any_c / jnp.sum(p, axis=0, keepdims=True))
        psum = psum + p
        o_cmp = jnp.dot(vct, p.astype(mxu_dtype), preferred_element_type=f32)
        out_ref_t[hh] = gates[hh][0:1] * o_cmp

    ovl = ovl_ref[...]
    imp = jnp.zeros((n_sel, tq), f32)
    for piece in _split3(psum):
        imp = imp + jnp.dot(ovl, piece, preferred_element_type=f32)
    blk = lax.broadcasted_iota(jnp.int32, (n_sel, 1), 0)
    cur = t_row // SEL_BLOCK
    causal_blk = blk * SEL_BLOCK <= t_row
    forced = (blk == 0) | ((blk <= cur) & (blk > cur - N_LOCAL_BLOCKS))
    imp = jnp.where(forced, -MASK_VALUE, jnp.where(causal_blk, imp, MASK_VALUE))
    cnt = jnp.zeros((n_sel, tq), f32)
    for i in range(n_sel):
        ri = imp[i:i + 1, :]
        beats = jnp.where(blk > i, (ri >= imp).astype(f32), (ri > imp).astype(f32))
        cnt = cnt + beats
    topk = min(SEL_TOPK, n_sel)
    selb_ref[...] = jnp.where(cnt < topk, 0.0, MASK_VALUE)

    d0 = (lax.broadcasted_iota(jnp.int32, (tk, tq), 1)
          - lax.broadcasted_iota(jnp.int32, (tk, tq), 0))
    d0f = d0.astype(f32)
    for hh in heads:
        alibi_ref[hh] = slopes[hh] * d0f
    blocks_per_chunk = tk // SEL_BLOCK

    def reset():
        acc_ref[...] = jnp.zeros_like(acc_ref)
        for hh in heads:
            ml_ref[2 * hh, 0:1, :] = jnp.full((1, tq), MASK_VALUE, f32)
            ml_ref[2 * hh + 1, 0:1, :] = jnp.zeros((1, tq), f32)

    def chunk(k_ref, vt_ref, k0, bias, delta_f):
        kblk = k_ref[0, pl.ds(k0, tk), :]
        vt = vt_ref[0, :, pl.ds(k0, tk)]
        scores = [lax.dot_general(kblk, q_ref[0, :, hh * hd:(hh + 1) * hd], _NT, preferred_element_type=f32)
                  for hh in heads]
        for hh in heads:
            s = scores[hh] - alibi_ref[hh]
            if bias is not None:
                s = s + bias
            cst = -slopes[hh] * delta_f
            m = ml_ref[2 * hh, 0:1, :]
            l = ml_ref[2 * hh + 1, 0:1, :]
            m_new = jnp.maximum(m, jnp.max(s, axis=0, keepdims=True) + cst)
            alpha = jnp.exp(m - m_new)
            p = jnp.exp(s - (m_new - cst))
            ml_ref[2 * hh, 0:1, :] = m_new
            ml_ref[2 * hh + 1, 0:1, :] = alpha * l + jnp.sum(p, axis=0, keepdims=True)
            acc_ref[hh] = alpha * acc_ref[hh] + jnp.dot(vt, p.astype(mxu_dtype), preferred_element_type=f32)

    def finish(branch):
        for hh in heads:
            out_ref_t[hh] += gates[hh][branch:branch + 1] * (acc_ref[hh] / ml_ref[2 * hh + 1, 0:1, :])

    def sel_rows(c):
        rows = [jnp.broadcast_to(selb_ref[pl.ds(c * blocks_per_chunk + j, 1), :], (SEL_BLOCK, tq))
                for j in range(blocks_per_chunk)]
        return jnp.concatenate(rows, axis=0)

    reset()

    def sel_body(c, carry):
        k0 = pl.multiple_of(c * tk, tk)
        chunk(ksel_ref, vselt_ref, k0, sel_rows(c), (q0 - k0).astype(f32))
        return carry

    lax.fori_loop(0, qi, sel_body, 0)
    k_diag = pl.multiple_of(q0, tk)
    chunk(ksel_ref, vselt_ref, k_diag, sel_rows(qi) + jnp.where(d0 >= 0, 0.0, MASK_VALUE), 0.0)
    finish(1)

    reset()
    for delta in range(0, WINDOW + tk, tk):
        lo_ok = delta - (tk - 1) >= 0
        hi_ok = delta + (tq - 1) < WINDOW
        if lo_ok and hi_ok:
            bias = None
        else:
            dist = d0 + delta
            bias = jnp.where(dist >= 0, jnp.where(dist < WINDOW, 0.0, MASK_VALUE), MASK_VALUE)
        if delta == 0:
            chunk(kwin_ref, vwint_ref, k_diag, bias, 0.0)
        else:
            @pl.when(q0 >= delta)
            def _(delta=delta, bias=bias):
                chunk(kwin_ref, vwint_ref, pl.multiple_of(q0 - delta, tk), bias, float(delta))
    finish(2)

    for hh in heads:
        o_ref[0, :, hh * hd:(hh + 1) * hd] = out_ref_t[hh].T.astype(o_ref.dtype)


def _nsa_stage(big, small, cmp, b, s, tq=256):
    g, hd = NSA_KV_GROUPS, HEAD_DIM
    tq = min(tq, s)
    tk = tq
    n_seg = s // CMP_STRIDE
    n_sel = s // SEL_BLOCK
    big3 = big.reshape(b, s, BIG_WIDTH)
    kv_col = lambda which: (COL_KV + which * NSA_KV_WIDTH) // hd
    vt = lambda which: big3[:, :, COL_KV + which * NSA_KV_WIDTH:COL_KV + (which + 1) * NSA_KV_WIDTH] \
        .reshape(b, s, g, hd).transpose(0, 2, 3, 1).reshape(b * g, hd, s)
    vsel_t, vwin_t = vt(3), vt(5)
    kc = cmp[0]
    vc_t = cmp[1].transpose(0, 2, 1)
    gate_t = small[:, SMALL_GATE:SMALL_GATE + 3 * NSA_HEADS].reshape(b, s, g, 3 * HEADS_PER_GROUP) \
        .transpose(0, 2, 3, 1)
    c_start = np.arange(n_seg)[None, :] * CMP_STRIDE
    s_start = np.arange(n_sel)[:, None] * SEL_BLOCK
    n_cmp = (s - CMP_BLOCK) // CMP_STRIDE + 1
    ovl = ((c_start < s_start + SEL_BLOCK) & (c_start + CMP_BLOCK > s_start)
           & (np.arange(n_seg)[None, :] < n_cmp)).astype(np.float32)
    ovl = jnp.asarray(ovl, jnp.bfloat16)
    slopes = jnp.asarray((2.0 ** (-8.0 * np.arange(1, NSA_HEADS + 1) / NSA_HEADS))
                         .reshape(g, HEADS_PER_GROUP), jnp.float32)
    gw = 3 * HEADS_PER_GROUP
    kernel = functools.partial(_nsa_kernel, tq=tq, tk=tk)
    return pl.pallas_call(
        kernel,
        grid=(b, g, s // tq),
        in_specs=[
            pl.BlockSpec(memory_space=pltpu.SMEM),
            pl.BlockSpec((1, tq, HEADS_PER_GROUP * hd), lambda bi, gi, qi: (bi, qi, gi)),
            pl.BlockSpec((1, n_seg, hd), lambda bi, gi, qi: (bi * NSA_KV_GROUPS + gi, 0, 0)),
            pl.BlockSpec((1, hd, n_seg), lambda bi, gi, qi: (bi * NSA_KV_GROUPS + gi, 0, 0)),
            pl.BlockSpec((1, s, hd), lambda bi, gi, qi: (bi, 0, kv_col(2) + gi)),
            pl.BlockSpec((1, hd, s), lambda bi, gi, qi: (bi * NSA_KV_GROUPS + gi, 0, 0)),
            pl.BlockSpec((1, s, hd), lambda bi, gi, qi: (bi, 0, kv_col(4) + gi)),
            pl.BlockSpec((1, hd, s), lambda bi, gi, qi: (bi * NSA_KV_GROUPS + gi, 0, 0)),
            pl.BlockSpec((1, 1, gw, tq), lambda bi, gi, qi: (bi, gi, 0, qi)),
            pl.BlockSpec((n_sel, n_seg), lambda bi, gi, qi: (0, 0)),
        ],
        out_specs=pl.BlockSpec((1, tq, HEADS_PER_GROUP * hd), lambda bi, gi, qi: (bi, qi, gi)),
        out_shape=jax.ShapeDtypeStruct((b, s, NSA_Q_WIDTH), big.dtype),
        scratch_shapes=[
            pltpu.VMEM((n_sel, tq), jnp.float32),
            pltpu.VMEM((HEADS_PER_GROUP, tk, tq), jnp.float32),
            pltpu.VMEM((HEADS_PER_GROUP, hd, tq), jnp.float32),
            pltpu.VMEM((2 * HEADS_PER_GROUP, 8, tq), jnp.float32),
            pltpu.VMEM((HEADS_PER_GROUP, hd, tq), jnp.float32),
        ],
        compiler_params=_params(("parallel", "parallel", "arbitrary")),
    )(slopes, big3, kc, vc_t, big3, vsel_t, big3, vwin_t, gate_t, ovl)


_TN = (((0,), (0,)), ((), ()))
_HI = lax.Precision.HIGHEST


def _gla_kernel(q_ref, k_ref, v_ref, og_ref, sm_ref, wa_ref, ba_ref, gn_ref, selw_ref, o_ref,
                state_ref, *, n_chunks):
    f32 = jnp.float32
    mxu_dtype = v_ref.dtype
    c_len, sub = GLA_CHUNK, GLA_SUB
    ns = c_len // sub
    dk = GLA_KEY_DIM

    @pl.when(pl.program_id(2) == 0)
    def _():
        state_ref[...] = jnp.zeros_like(state_ref)

    tc = n_chunks * c_len
    chunks = range(n_chunks)
    row = lax.broadcasted_iota(jnp.int32, (c_len, 1), 0)
    col = lax.broadcasted_iota(jnp.int32, (1, c_len), 1)
    sub_row = row // sub
    sub_col = col // sub
    t_loc = lax.broadcasted_iota(jnp.int32, (1, sub, 1), 1)
    neg_inf = -jnp.inf

    logits = jnp.dot(sm_ref[0], wa_ref[0], precision=_HI, preferred_element_type=f32) + ba_ref[0]
    glog = (jnp.minimum(logits, 0.0) - jnp.log(1.0 + jnp.exp(-jnp.abs(logits)))) * (1.0 / GATE_TAU)
    row_b = lax.broadcasted_iota(jnp.int32, (tc, 1), 0)
    col_b = lax.broadcasted_iota(jnp.int32, (1, tc), 1)
    block_tril = jnp.where(col_b <= row_b, (row_b // c_len == col_b // c_len).astype(f32), 0.0)
    block_tril = block_tril.astype(jnp.bfloat16)
    bcum_all = jnp.zeros((tc, dk), f32)
    for piece in _split3(glog):
        bcum_all = bcum_all + jnp.dot(block_tril, piece, preferred_element_type=f32)

    rows = [slice(c * c_len, (c + 1) * c_len) for c in chunks]
    qs = [q_ref[0, rows[c], :].astype(f32) for c in chunks]
    ks = [k_ref[0, rows[c], :].astype(f32) for c in chunks]
    vs = [v_ref[0, rows[c], :] for c in chunks]
    bcums = [bcum_all[rows[c], :] for c in chunks]
    b_lasts = [b[c_len - 1:c_len, :] for b in bcums]

    q_ins, kfs, qfs, ecats, kds = [], [], [], [], []
    for c in chunks:
        q, k, bcum = qs[c], ks[c], bcums[c]
        q_ins.append((q * jnp.exp(bcum)).astype(mxu_dtype))
        r = [bcum[(j + 1) * sub - 1:(j + 1) * sub, :] for j in range(ns)]
        r_rows = jnp.concatenate([jnp.broadcast_to(rj, (sub, dk)) for rj in r], axis=0)
        kfs.append((k * jnp.exp(r_rows - bcum)).astype(mxu_dtype))
        qfs.append([(q * jnp.exp(jnp.where(row >= (j + 1) * sub, bcum - r[j], neg_inf))).astype(mxu_dtype)
                    for j in range(ns - 1)])
        q3 = q.reshape(ns, sub, dk)
        k3 = k.reshape(ns, sub, dk)
        b3 = bcum.reshape(ns, sub, dk)
        pieces = []
        for s_ in range(sub):
            arg = jnp.where(t_loc >= s_, b3 - b3[:, s_:s_ + 1, :], neg_inf)
            e = q3 * k3[:, s_:s_ + 1, :] * jnp.exp(arg)
            pieces.append(e.reshape(c_len, dk).astype(mxu_dtype))
        ecats.append(jnp.concatenate(pieces, axis=1))
        kds.append((k * jnp.exp(b_lasts[c] - bcum)).astype(mxu_dtype))

    ajs = [[lax.dot_general(qfs[c][j], kfs[c], _NT, preferred_element_type=f32) for j in range(ns - 1)]
           for c in chunks]
    d_wides = [jnp.dot(ecats[c], selw_ref[...], preferred_element_type=f32) for c in chunks]
    updates = [lax.dot_general(vs[c], kds[c], _TN, preferred_element_type=f32) for c in chunks]
    o_intras = []
    for c in chunks:
        a = jnp.where(sub_row == sub_col, d_wides[c], 0.0)
        for j in range(ns - 1):
            a = a + jnp.where(sub_col == j, ajs[c][j], 0.0)
        o_intras.append(jnp.dot(a.astype(mxu_dtype), vs[c], preferred_element_type=f32))

    st = state_ref[...]
    for c in chunks:
        o = o_intras[c] + lax.dot_general(q_ins[c], st.astype(mxu_dtype), _NT, preferred_element_type=f32)
        st = st * jnp.exp(b_lasts[c]) + updates[c]
        rms = lax.rsqrt(jnp.mean(o * o, axis=-1, keepdims=True) + EPS)
        gate = og_ref[0, rows[c], :].astype(f32)
        y = o * rms * gn_ref[...] * (gate * jax.nn.sigmoid(gate))
        o_ref[0, rows[c], :] = y.astype(o_ref.dtype)
    state_ref[...] = st


def _gla_stage(big, small, w_alpha2, b_alpha, g_norm, b, s, tc=512):
    h, dk, dv = GLA_HEADS, GLA_KEY_DIM, GLA_VAL_DIM
    tc = min(tc, s)
    big3 = big.reshape(b, s, BIG_WIDTH)
    small3 = small.reshape(b, s, SMALL_WIDTH)
    wa = jnp.zeros((h, SMALL_WIDTH, dk), jnp.float32).at[:, SMALL_ALPHA:SMALL_ALPHA + GATE_RANK, :].set(
        w_alpha2.astype(jnp.float32).reshape(GATE_RANK, h, dk).transpose(1, 0, 2))
    ba = b_alpha.astype(jnp.float32).reshape(h, 1, dk)
    gn = g_norm.astype(jnp.float32).reshape(1, dv)
    sel = (np.arange(GLA_SUB * dk)[:, None] // dk == np.arange(GLA_CHUNK)[None, :] % GLA_SUB)
    selw = jnp.asarray(sel.astype(np.float32), big.dtype)
    kernel = functools.partial(_gla_kernel, n_chunks=tc // GLA_CHUNK)
    return pl.pallas_call(
        kernel,
        grid=(b, h, s // tc),
        in_specs=[
            pl.BlockSpec((1, tc, dk), lambda bi, hi, ci: (bi, ci, COL_GLA_Q // dk + hi)),
            pl.BlockSpec((1, tc, dk), lambda bi, hi, ci: (bi, ci, COL_GLA_K // dk + hi)),
            pl.BlockSpec((1, tc, dv), lambda bi, hi, ci: (bi, ci, COL_GLA_V // dv + hi)),
            pl.BlockSpec((1, tc, dv), lambda bi, hi, ci: (bi, ci, COL_GLA_GATE // dv + hi)),
            pl.BlockSpec((1, tc, SMALL_WIDTH), lambda bi, hi, ci: (bi, ci, 0)),
            pl.BlockSpec((1, SMALL_WIDTH, dk), lambda bi, hi, ci: (hi, 0, 0)),
            pl.BlockSpec((1, 1, dk), lambda bi, hi, ci: (hi, 0, 0)),
            pl.BlockSpec((1, dv), lambda bi, hi, ci: (0, 0)),
            pl.BlockSpec((GLA_SUB * dk, GLA_CHUNK), lambda bi, hi, ci: (0, 0)),
        ],
        out_specs=pl.BlockSpec((1, tc, dv), lambda bi, hi, ci: (bi, ci, hi)),
        out_shape=jax.ShapeDtypeStruct((b, s, GLA_V_WIDTH), big.dtype),
        scratch_shapes=[pltpu.VMEM((dv, dk), jnp.float32)],
        compiler_params=_params(("parallel", "parallel", "arbitrary")),
    )(big3, big3, big3, big3, small3, wa, ba, gn, selw)


ROUTE_E0 = N_EXPERT_GROUPS
ROUTE_ID = ROUTE_E0 + N_EXPERTS
ROUTE_W = ROUTE_ID + 2
MOE_TM = 256
GATHER_UNROLL = 8


def _out_proj_kernel(x_ref, nsa_ref, gla_ref, wt_ref, wb_ref, g_ref, wr_ref, br_ref,
                     x1_ref, h_ref, comb_ref):
    f32 = jnp.float32
    acc = jnp.dot(nsa_ref[...], wt_ref[...], preferred_element_type=f32)
    acc = acc + jnp.dot(gla_ref[...], wb_ref[...], preferred_element_type=f32)
    x1 = x_ref[...] + acc
    x1_ref[...] = x1
    hf = x1 * lax.rsqrt(jnp.mean(x1 * x1, axis=-1, keepdims=True) + EPS) * g_ref[...]
    h_ref[...] = hf.astype(h_ref.dtype)

    h_hi, h_mid, _ = _split3(hf)
    logits = (jnp.dot(h_hi, wr_ref[0], preferred_element_type=f32)
              + jnp.dot(h_hi, wr_ref[1], preferred_element_type=f32)
              + jnp.dot(h_mid, wr_ref[0], preferred_element_type=f32)) + br_ref[...]

    lane = lax.broadcasted_iota(jnp.int32, (1, LANES), 1).astype(f32)
    big_lane = float(LANES)
    neg_inf = -jnp.inf
    first_argmax = lambda vals, vmax: jnp.min(jnp.where(vals == vmax, lane, big_lane), axis=-1, keepdims=True)
    gl = jnp.where(lane < N_EXPERT_GROUPS, logits, neg_inf)
    gmax = jnp.max(gl, axis=-1, keepdims=True)
    gsel = first_argmax(gl, gmax)
    gw = 1.0 / jnp.sum(jnp.exp(gl - gmax), axis=-1, keepdims=True)
    lo = ROUTE_E0 + EXPERTS_PER_GROUP * gsel
    el = jnp.where(lane >= lo, jnp.where(lane < lo + EXPERTS_PER_GROUP, logits, neg_inf), neg_inf)
    m1 = jnp.max(el, axis=-1, keepdims=True)
    i1 = first_argmax(el, m1)
    el2 = jnp.where(lane == i1, neg_inf, el)
    m2 = jnp.max(el2, axis=-1, keepdims=True)
    i2 = first_argmax(el2, m2)
    e2 = jnp.exp(m2 - m1)
    w1 = gw / (1.0 + e2)
    w2 = w1 * e2
    onehot = jnp.where(lane == i1, 1.0, 0.0) + jnp.where(lane == i2, 1.0, 0.0)
    meta = (jnp.where(lane == ROUTE_ID, i1 - ROUTE_E0, 0.0) + jnp.where(lane == ROUTE_ID + 1, i2 - ROUTE_E0, 0.0)
            + jnp.where(lane == ROUTE_W, w1, 0.0) + jnp.where(lane == ROUTE_W + 1, w2, 0.0))
    comb_ref[...] = onehot + meta


def _out_proj_stage(x2, nsa, gla, w_out, g_ffn, w_rg, b_rg, w_re, b_re, tm=512):
    t, d = x2.shape
    tm = min(tm, t)
    half = nsa.shape[1]
    w_top = w_out[:half].astype(nsa.dtype)
    w_bot = w_out[half:].astype(nsa.dtype)
    wr = jnp.concatenate([w_rg, w_re.reshape(d, N_EXPERTS),
                          jnp.zeros((d, LANES - ROUTE_E0 - N_EXPERTS), jnp.float32)], axis=1)
    wr_hi, wr_mid, _ = _split3(wr)
    wr2 = jnp.stack([wr_hi, wr_mid])
    br = jnp.concatenate([b_rg, b_re.reshape(N_EXPERTS),
                          jnp.zeros((LANES - ROUTE_E0 - N_EXPERTS,), jnp.float32)])[None]
    row = lambda m: (m, 0)
    fixed = lambda m: (0, 0)
    return pl.pallas_call(
        _out_proj_kernel,
        grid=(t // tm,),
        in_specs=[
            pl.BlockSpec((tm, d), row),
            pl.BlockSpec((tm, half), row),
            pl.BlockSpec((tm, half), row),
            pl.BlockSpec((half, d), fixed),
            pl.BlockSpec((half, d), fixed),
            pl.BlockSpec((1, d), fixed),
            pl.BlockSpec((2, d, LANES), lambda m: (0, 0, 0)),
            pl.BlockSpec((1, LANES), fixed),
        ],
        out_specs=[pl.BlockSpec((tm, d), row), pl.BlockSpec((tm, d), row), pl.BlockSpec((tm, LANES), row)],
        out_shape=[
            jax.ShapeDtypeStruct((t, d), jnp.float32),
            jax.ShapeDtypeStruct((t, d), jnp.float32),
            jax.ShapeDtypeStruct((t, LANES), jnp.float32),
        ],
        compiler_params=_params(("parallel",)),
    )(x2, nsa, gla, w_top, w_bot, g_ffn.astype(jnp.float32)[None], wr2, br)


def _rank_kernel(route_ref, base_ref, rank_ref, count_ref, carry_ref):
    f32 = jnp.float32
    tm = route_ref.shape[0]

    @pl.when(pl.program_id(0) == 0)
    def _():
        carry_ref[...] = base_ref[...]

    route = route_ref[...]
    lane = lax.broadcasted_iota(jnp.int32, (1, LANES), 1)
    lane_f = lane.astype(f32)
    onehot = jnp.where((lane >= ROUTE_E0) & (lane < ROUTE_ID), route, 0.0)
    row = lax.broadcasted_iota(jnp.int32, (tm, tm), 0)
    col = lax.broadcasted_iota(jnp.int32, (tm, tm), 1)
    strict_lower = (col < row).astype(jnp.bfloat16)
    before = jnp.dot(strict_lower, onehot.astype(jnp.bfloat16), preferred_element_type=f32) + carry_ref[0:1, :]
    ranks = []
    for j in range(2):
        lane_j = route[:, ROUTE_ID + j:ROUTE_ID + j + 1] + float(ROUTE_E0)
        ranks.append(jnp.sum(jnp.where(lane_f == lane_j, before, 0.0), axis=-1, keepdims=True))
    rank_ref[...] = jnp.where(lane == 0, ranks[0], jnp.where(lane == 1, ranks[1], 0.0))
    total = carry_ref[0:1, :] + jnp.sum(onehot, axis=0, keepdims=True)
    carry_ref[...] = jnp.broadcast_to(total, carry_ref.shape)
    count_ref[...] = jnp.broadcast_to(total, count_ref.shape)


def _rank_stage(route, base, tm=512):
    t = route.shape[0]
    tm = min(tm, t)
    return pl.pallas_call(
        _rank_kernel,
        grid=(t // tm,),
        in_specs=[pl.BlockSpec((tm, LANES), lambda m: (m, 0)), pl.BlockSpec((8, LANES), lambda m: (0, 0))],
        out_specs=[pl.BlockSpec((tm, LANES), lambda m: (m, 0)), pl.BlockSpec((8, LANES), lambda m: (0, 0))],
        out_shape=[jax.ShapeDtypeStruct((t, LANES), jnp.float32), jax.ShapeDtypeStruct((8, LANES), jnp.float32)],
        scratch_shapes=[pltpu.VMEM((8, LANES), jnp.float32)],
        compiler_params=_params(("arbitrary",)),
    )(route, base)


def _gather_rows(idx_ref, n_rows, src_hbm, dst, sem):
    def body(r, carry):
        tok = idx_ref[0, 0, r]
        pltpu.make_async_copy(src_hbm.at[pl.ds(tok, 1), :], dst.at[pl.ds(r, 1), :], sem).start()
        return carry
    lax.fori_loop(0, n_rows, body, 0, unroll=GATHER_UNROLL)


def _wait_rows(n_rows, src_hbm, dst, sem):
    pltpu.make_async_copy(src_hbm.at[pl.ds(0, n_rows), :], dst, sem).wait()


def _moe_kernel(te_ref, na_ref, idx_ref, idx_next_ref, h_hbm, wg_ref, wu_ref, wd_ref, y_ref,
                buf, sem, wg_bf, wu_bf, wd_bf):
    f32 = jnp.float32
    bf16 = jnp.bfloat16
    i = pl.program_id(0)
    n_active = na_ref[0]
    tm = buf.shape[1]
    slot = i % 2

    @pl.when(i == 0)
    def _():
        _gather_rows(idx_ref, tm, h_hbm, buf.at[0], sem.at[0])

    @pl.when(i < n_active)
    def _():
        expert_changed = (i == 0) | (te_ref[i] != te_ref[jnp.maximum(i - 1, 0)])

        @pl.when(expert_changed)
        def _():
            wg_bf[...] = wg_ref[0].astype(bf16)
            wu_bf[...] = wu_ref[0].astype(bf16)
            wd_bf[...] = wd_ref[0].astype(bf16)

        for cur in range(2):
            @pl.when(slot == cur)
            def _(cur=cur):
                nxt = 1 - cur
                _wait_rows(tm, h_hbm, buf.at[cur], sem.at[cur])
                for r in range(tm):
                    tok = idx_next_ref[0, 0, r]
                    pltpu.make_async_copy(h_hbm.at[pl.ds(tok, 1), :], buf.at[nxt, pl.ds(r, 1), :],
                                          sem.at[nxt]).start()
                xb = buf[cur].astype(bf16)
                gate = jnp.dot(xb, wg_bf[...], preferred_element_type=f32)
                up = jnp.dot(xb, wu_bf[...], preferred_element_type=f32)
                hid = (gate * jax.nn.sigmoid(gate) * up).astype(bf16)
                y_ref[...] = jnp.dot(hid, wd_bf[...], preferred_element_type=f32)

    @pl.when(i == n_active)
    def _():
        _wait_rows(tm, h_hbm, buf.at[slot], sem.at[slot])

    @pl.when(i >= n_active)
    def _():
        y_ref[...] = jnp.zeros_like(y_ref)


def _moe_stage(h2, tok_of_pos, tile_expert, n_active, w_gate, w_up, w_down):
    t, d = h2.shape
    n_tiles, _, tm = tok_of_pos.shape
    f = w_gate.shape[-1]
    wg = w_gate.reshape(N_EXPERTS, d, f)
    wu = w_up.reshape(N_EXPERTS, d, f)
    wd = w_down.reshape(N_EXPERTS, f, d)
    grid_spec = pltpu.PrefetchScalarGridSpec(
        num_scalar_prefetch=2,
        grid=(n_tiles,),
        in_specs=[
            pl.BlockSpec((1, 1, tm), lambda i, te, na: (i, 0, 0), memory_space=pltpu.SMEM),
            pl.BlockSpec((1, 1, tm), lambda i, te, na: (jnp.minimum(i + 1, n_tiles - 1), 0, 0),
                         memory_space=pltpu.SMEM),
            pl.BlockSpec(memory_space=pl.ANY),
            pl.BlockSpec((1, d, f), lambda i, te, na: (te[i], 0, 0)),
            pl.BlockSpec((1, d, f), lambda i, te, na: (te[i], 0, 0)),
            pl.BlockSpec((1, f, d), lambda i, te, na: (te[i], 0, 0)),
        ],
        out_specs=pl.BlockSpec((tm, d), lambda i, te, na: (i, 0)),
        scratch_shapes=[
            pltpu.VMEM((2, tm, d), jnp.float32), pltpu.SemaphoreType.DMA((2,)),
            pltpu.VMEM((d, f), jnp.bfloat16), pltpu.VMEM((d, f), jnp.bfloat16), pltpu.VMEM((f, d), jnp.bfloat16),
        ],
    )
    return pl.pallas_call(
        _moe_kernel,
        grid_spec=grid_spec,
        out_shape=jax.ShapeDtypeStruct((n_tiles * tm, d), jnp.float32),
        compiler_params=_params(("arbitrary",)),
    )(tile_expert, n_active, tok_of_pos, tok_of_pos, h2, wg, wu, wd)


def _combine_kernel(p1_ref, p2_ref, p1n_ref, p2n_ref, route_ref, x1_ref, gf_ref, y_hbm, o_ref, buf, sem,
                    *, final_norm):
    i = pl.program_id(0)
    n = pl.num_programs(0)
    tm = buf.shape[2]

    def gather(a_ref, b_ref, slot):
        for j, idx_ref in enumerate((a_ref, b_ref)):
            for r in range(tm):
                pltpu.make_async_copy(y_hbm.at[pl.ds(idx_ref[0, 0, r], 1), :],
                                      buf.at[slot, j, pl.ds(r, 1), :], sem.at[slot, j]).start()

    @pl.when(i == 0)
    def _():
        gather(p1_ref, p2_ref, 0)

    @pl.when(i + 1 < n)
    def _():
        gather(p1n_ref, p2n_ref, (i + 1) % 2)

    slot = i % 2
    _wait_rows(tm, y_hbm, buf.at[slot, 0], sem.at[slot, 0])
    _wait_rows(tm, y_hbm, buf.at[slot, 1], sem.at[slot, 1])
    route = route_ref[...]
    w1 = route[:, ROUTE_W:ROUTE_W + 1]
    w2 = route[:, ROUTE_W + 1:ROUTE_W + 2]
    y = x1_ref[...] + (w1 * buf[slot, 0] + w2 * buf[slot, 1])
    if final_norm:
        y = y * lax.rsqrt(jnp.mean(y * y, axis=-1, keepdims=True) + EPS) * gf_ref[...]
    o_ref[...] = y


def _combine_stage(y_sorted, pos1, pos2, route, x1, g_final, final_norm, tm=256):
    t, d = x1.shape
    tm = min(tm, t)
    n = t // tm
    p1 = pos1.reshape(n, 1, tm)
    p2 = pos2.reshape(n, 1, tm)
    cur = lambda i: (i, 0, 0)
    nxt = lambda i: (jnp.minimum(i + 1, n - 1), 0, 0)
    smem = lambda index_map: pl.BlockSpec((1, 1, tm), index_map, memory_space=pltpu.SMEM)
    return pl.pallas_call(
        functools.partial(_combine_kernel, final_norm=final_norm),
        grid=(n,),
        in_specs=[
            smem(cur), smem(cur), smem(nxt), smem(nxt),
            pl.BlockSpec((tm, LANES), lambda i: (i, 0)),
            pl.BlockSpec((tm, d), lambda i: (i, 0)),
            pl.BlockSpec((1, d), lambda i: (0, 0)),
            pl.BlockSpec(memory_space=pl.ANY),
        ],
        out_specs=pl.BlockSpec((tm, d), lambda i: (i, 0)),
        out_shape=jax.ShapeDtypeStruct((t, d), jnp.float32),
        scratch_shapes=[pltpu.VMEM((2, 2, tm, d), jnp.float32), pltpu.SemaphoreType.DMA((2, 2))],
        compiler_params=_params(("arbitrary",)),
    )(p1, p2, p1, p2, route, x1, g_final.astype(jnp.float32)[None], y_sorted)


def _routing_tables(route, tm):
    t = route.shape[0]
    zero_base = jnp.zeros((8, LANES), jnp.float32)
    _, count = _rank_stage(route, zero_base)
    counts = count[0, ROUTE_E0:ROUTE_E0 + N_EXPERTS].astype(jnp.int32)
    padded = ((counts + tm - 1) // tm) * tm
    ends = jnp.cumsum(padded)
    offs = ends - padded
    base = zero_base.at[:, ROUTE_E0:ROUTE_E0 + N_EXPERTS].set(
        jnp.broadcast_to(offs.astype(jnp.float32), (8, N_EXPERTS)))
    pos, _ = _rank_stage(route, base)
    pos1 = pos[:, 0].astype(jnp.int32)
    pos2 = pos[:, 1].astype(jnp.int32)
    n_tiles = (2 * t) // tm + N_EXPERTS
    tok = jnp.arange(t, dtype=jnp.int32)
    tok_of_pos = jnp.zeros((n_tiles * tm,), jnp.int32).at[jnp.concatenate([pos1, pos2])].set(
        jnp.concatenate([tok, tok]), unique_indices=True)
    tile_start = jnp.arange(n_tiles, dtype=jnp.int32) * tm
    tile_expert = jnp.minimum(jnp.sum((ends[None, :] <= tile_start[:, None]).astype(jnp.int32), axis=1),
                              N_EXPERTS - 1)
    n_active = (ends[-1] // tm).astype(jnp.int32).reshape(1)
    return pos1, pos2, tok_of_pos.reshape(n_tiles, 1, tm), tile_expert, n_active


def kernel(x, g_mix_norm, w_in, b_nsa_gate, cmp_pos_k, w_cmp_k1, w_cmp_k2, cmp_pos_v, w_cmp_v1, w_cmp_v2,
           w_alpha2, b_alpha, g_gla_norm, w_out, g_ffn_norm, w_router_group, b_router_group,
           w_router_expert, b_router_expert, w_expert_gate, w_expert_up, w_expert_down, g_final_norm):
    b, s, d = x.shape
    depth = w_in.shape[0]
    x2 = x.reshape(b * s, d)
    for l in range(depth):
        last = l == depth - 1
        w_big, w_small, col_scale, small_bias = _prep_in_proj_weights(w_in[l], b_nsa_gate[l])
        big, small = _in_proj(x2, g_mix_norm[l].astype(jnp.float32)[None], w_big, w_small, col_scale, small_bias)
        cmp = _compress_stage(big, b, s, cmp_pos_k[l], w_cmp_k1[l], w_cmp_k2[l],
                              cmp_pos_v[l], w_cmp_v1[l], w_cmp_v2[l])
        nsa = _nsa_stage(big, small, cmp, b, s).reshape(b * s, NSA_Q_WIDTH)
        gla = _gla_stage(big, small, w_alpha2[l], b_alpha[l], g_gla_norm[l], b, s).reshape(b * s, GLA_V_WIDTH)
        x1, h2, route = _out_proj_stage(x2, nsa, gla, w_out[l], g_ffn_norm[l], w_router_group[l],
                                        b_router_group[l], w_router_expert[l], b_router_expert[l])
        pos1, pos2, tok_of_pos, tile_expert, n_active = _routing_tables(route, MOE_TM)
        y_sorted = _moe_stage(h2, tok_of_pos, tile_expert, n_active,
                              w_expert_gate[l], w_expert_up[l], w_expert_down[l])
        x2 = _combine_stage(y_sorted, pos1, pos2, route, x1, g_final_norm, last)
    return x2.reshape(b, s, d)
```

```python
import functools

import numpy as np
import jax
import jax.numpy as jnp
from jax import lax
from jax.experimental import pallas as pl
from jax.experimental.pallas import tpu as pltpu

NSA_HEADS = 8
NSA_KV_GROUPS = 2
HEADS_PER_GROUP = NSA_HEADS // NSA_KV_GROUPS
HEAD_DIM = 128
CMP_BLOCK = 32
CMP_STRIDE = 16
SEL_BLOCK = 64
SEL_TOPK = 8
N_LOCAL_BLOCKS = 2
WINDOW = 512
GLA_HEADS = 4
GLA_KEY_DIM = 128
GLA_VAL_DIM = 256
GATE_RANK = 16
GATE_TAU = 16.0
GLA_CHUNK = 64
GLA_SUB = 16
N_EXPERT_GROUPS = 4
EXPERTS_PER_GROUP = 8
N_EXPERTS = N_EXPERT_GROUPS * EXPERTS_PER_GROUP
EPS = 1e-6
MASK_VALUE = -1e30

NSA_Q_WIDTH = NSA_HEADS * HEAD_DIM
NSA_KV_WIDTH = NSA_KV_GROUPS * HEAD_DIM
GLA_QK_WIDTH = GLA_HEADS * GLA_KEY_DIM
GLA_V_WIDTH = GLA_HEADS * GLA_VAL_DIM

LANES = 128
VMEM_LIMIT_BYTES = 56 * 1024 * 1024

COL_NSA_Q = 0
COL_KV = COL_NSA_Q + NSA_Q_WIDTH
COL_GLA_Q = COL_KV + 6 * NSA_KV_WIDTH
COL_GLA_K = COL_GLA_Q + GLA_QK_WIDTH
COL_GLA_V = COL_GLA_K + GLA_QK_WIDTH
COL_GLA_GATE = COL_GLA_V + GLA_V_WIDTH
BIG_WIDTH = COL_GLA_GATE + GLA_V_WIDTH
SMALL_GATE = 0
SMALL_ALPHA = 3 * NSA_HEADS
SMALL_WIDTH = LANES


def _params(semantics):
    return pltpu.CompilerParams(dimension_semantics=semantics, vmem_limit_bytes=VMEM_LIMIT_BYTES)


def _split3(a):
    hi = a.astype(jnp.bfloat16)
    r1 = a - hi.astype(jnp.float32)
    mid = r1.astype(jnp.bfloat16)
    lo = (r1 - mid.astype(jnp.float32)).astype(jnp.bfloat16)
    return hi, mid, lo


def _in_proj_kernel(x_ref, g_ref, wb_ref, ws_ref, scale_ref, sbias_ref, big_ref, small_ref, h_ref):
    n = pl.program_id(1)

    @pl.when(n == 0)
    def _():
        xf = x_ref[...]
        y = xf * lax.rsqrt(jnp.mean(xf * xf, axis=-1, keepdims=True) + EPS)
        h = (y * g_ref[...]).astype(jnp.bfloat16)
        h_ref[...] = h
        small_ref[...] = jnp.dot(h, ws_ref[...], preferred_element_type=jnp.float32) + sbias_ref[...]

    acc = jnp.dot(h_ref[...], wb_ref[...], preferred_element_type=jnp.float32)
    big_ref[...] = (acc * scale_ref[...]).astype(big_ref.dtype)


def _in_proj(x2, gain, w_big, w_small, col_scale, small_bias, tm=1024, tn=1408):
    t, d = x2.shape
    tm = min(tm, t)
    grid = (t // tm, BIG_WIDTH // tn)
    return pl.pallas_call(
        _in_proj_kernel,
        grid=grid,
        in_specs=[
            pl.BlockSpec((tm, d), lambda m, n: (m, 0)),
            pl.BlockSpec((1, d), lambda m, n: (0, 0)),
            pl.BlockSpec((d, tn), lambda m, n: (0, n)),
            pl.BlockSpec((d, SMALL_WIDTH), lambda m, n: (0, 0)),
            pl.BlockSpec((1, tn), lambda m, n: (0, n)),
            pl.BlockSpec((1, SMALL_WIDTH), lambda m, n: (0, 0)),
        ],
        out_specs=[
            pl.BlockSpec((tm, tn), lambda m, n: (m, n)),
            pl.BlockSpec((tm, SMALL_WIDTH), lambda m, n: (m, 0)),
        ],
        out_shape=[
            jax.ShapeDtypeStruct((t, BIG_WIDTH), jnp.bfloat16),
            jax.ShapeDtypeStruct((t, SMALL_WIDTH), jnp.float32),
        ],
        scratch_shapes=[pltpu.VMEM((tm, d), jnp.bfloat16)],
        compiler_params=_params(("parallel", "arbitrary")),
    )(x2, gain, w_big, w_small, col_scale, small_bias)


def _prep_in_proj_weights(w_in, b_nsa_gate):
    sizes = (NSA_Q_WIDTH,) + (NSA_KV_WIDTH,) * 6 + (3 * NSA_HEADS, GLA_QK_WIDTH, GLA_QK_WIDTH,
                                                    GLA_V_WIDTH, GATE_RANK, GLA_V_WIDTH)
    offs = np.concatenate([[0], np.cumsum(sizes)])
    part = lambda i: w_in[:, offs[i]:offs[i + 1]]
    w_big = jnp.concatenate([part(i) for i in (0, 1, 2, 3, 4, 5, 6, 8, 9, 10, 12)], axis=1)
    pad = jnp.zeros((w_in.shape[0], SMALL_WIDTH - 3 * NSA_HEADS - GATE_RANK), w_in.dtype)
    w_small = jnp.concatenate([part(7), part(11), pad], axis=1)
    scale = np.ones((1, BIG_WIDTH), np.float32)
    scale[0, COL_NSA_Q:COL_NSA_Q + NSA_Q_WIDTH] = HEAD_DIM ** -0.5
    scale[0, COL_GLA_Q:COL_GLA_Q + GLA_QK_WIDTH] = GLA_KEY_DIM ** -0.5
    small_bias = jnp.concatenate([b_nsa_gate.astype(jnp.float32),
                                  jnp.zeros((SMALL_WIDTH - 3 * NSA_HEADS,), jnp.float32)])[None]
    return w_big.astype(jnp.bfloat16), w_small.astype(jnp.bfloat16), jnp.asarray(scale), small_bias


def _gelu_tanh(a):
    return 0.5 * a * (1.0 + jnp.tanh(np.sqrt(2.0 / np.pi) * (a + 0.044715 * (a * a * a))))


def _compress_kernel(z_ref, pos_ref, w1_ref, w2_ref, o_ref):
    z = z_ref[0].astype(jnp.float32)
    pos = pos_ref[0]
    z_lo = (z + pos[0:1]).astype(jnp.bfloat16)
    z_hi = (z + pos[1:2]).astype(jnp.bfloat16)
    u = jnp.dot(z_lo, w1_ref[0, 0], preferred_element_type=jnp.float32)
    v = jnp.dot(z_hi, w1_ref[0, 1], preferred_element_type=jnp.float32)
    a = u + pltpu.roll(v, v.shape[0] - 1, axis=0)
    hid = _gelu_tanh(a).astype(jnp.bfloat16)
    o_ref[0] = jnp.dot(hid, w2_ref[0], preferred_element_type=jnp.float32).astype(o_ref.dtype)


def _compress(z, pos, w1, w2, tr):
    _, rows, width = z.shape
    hid = w1.shape[-1]
    return pl.pallas_call(
        _compress_kernel,
        grid=(2, rows // tr),
        in_specs=[
            pl.BlockSpec((1, tr, width), lambda c, r: (c, r, 0)),
            pl.BlockSpec((1, 2, width), lambda c, r: (c, 0, 0)),
            pl.BlockSpec((1, 2, width, hid), lambda c, r: (c, 0, 0, 0)),
            pl.BlockSpec((1, hid, HEAD_DIM), lambda c, r: (c, 0, 0)),
        ],
        out_specs=pl.BlockSpec((1, tr, HEAD_DIM), lambda c, r: (c, r, 0)),
        out_shape=jax.ShapeDtypeStruct((2, rows, HEAD_DIM), jnp.bfloat16),
        compiler_params=_params(("parallel", "parallel")),
    )(z, pos, w1, w2)


def _compress_stage(big, b, s, cmp_pos_k, w_cmp_k1, w_cmp_k2, cmp_pos_v, w_cmp_v1, w_cmp_v2):
    g, dk = NSA_KV_GROUPS, HEAD_DIM
    n_seg = s // CMP_STRIDE
    kv = big[:, COL_KV:COL_KV + 2 * NSA_KV_WIDTH].reshape(b, n_seg, CMP_STRIDE, 2, g, dk)
    z = kv.transpose(3, 0, 4, 1, 2, 5).reshape(2, b * g * n_seg, CMP_STRIDE * dk)
    pos = jnp.stack([cmp_pos_k, cmp_pos_v]).reshape(2, 2, CMP_STRIDE * dk)
    w1 = jnp.stack([w_cmp_k1, w_cmp_v1]).reshape(2, 2, CMP_STRIDE * dk, -1).astype(jnp.bfloat16)
    w2 = jnp.stack([w_cmp_k2, w_cmp_v2]).astype(jnp.bfloat16)
    pairs = b * g
    per_tile = max(p for p in (1, 2, 4, 8) if pairs % p == 0 and p * n_seg <= 1024)
    out = _compress(z, pos, w1, w2, per_tile * n_seg)
    return out.reshape(2, b * g, n_seg, dk)


_NT = (((1,), (1,)), ((), ()))


def _nsa_kernel(slopes_ref, q_ref, kc_ref, vct_ref, ksel_ref, vselt_ref, kwin_ref, vwint_ref,
                gate_ref, ovl_ref, o_ref, selb_ref, alibi_ref, acc_ref, ml_ref, out_ref_t, *, tq, tk):
    g = pl.program_id(1)
    qi = pl.program_id(2)
    q0 = qi * tq
    n_seg = kc_ref.shape[1]
    n_sel = ovl_ref.shape[0]
    mxu_dtype = ksel_ref.dtype
    f32 = jnp.float32
    hd = HEAD_DIM

    t_row = q0 + lax.broadcasted_iota(jnp.int32, (1, tq), 1)

    cmp_end = lax.broadcasted_iota(jnp.int32, (n_seg, 1), 0) * CMP_STRIDE + (CMP_BLOCK - 1)
    valid_c = cmp_end <= t_row
    dist_c = (t_row - cmp_end).astype(f32)
    any_c = (t_row >= CMP_BLOCK - 1).astype(f32)
    kc = kc_ref[0]
    vct = vct_ref[0]
    psum = jnp.zeros((n_seg, tq), f32)
    heads = range(HEADS_PER_GROUP)
    slopes = [slopes_ref[g, hh] for hh in heads]
    gates = [jax.nn.sigmoid(gate_ref[0, 0, 3 * hh:3 * hh + 3, :]) for hh in heads]
    for hh in heads:
        qh = q_ref[0, :, hh * hd:(hh + 1) * hd]
        sc = lax.dot_general(kc, qh, _NT, preferred_element_type=f32)
        sc = jnp.where(valid_c, sc - slopes[hh] * dist_c, MASK_VALUE)
        m = jnp.max(sc, axis=0, keepdims=True)
        p = jnp.exp(sc - m)
        p = p * (any_c / jnp.sum(p, axis=0, keepdims=True))
        psum = psum + p
        o_cmp = jnp.dot(vct, p.astype(mxu_dtype), preferred_element_type=f32)
        out_ref_t[hh] = gates[hh][0:1] * o_cmp

    ovl = ovl_ref[...]
    imp = jnp.zeros((n_sel, tq), f32)
    for piece in _split3(psum):
        imp = imp + jnp.dot(ovl, piece, preferred_element_type=f32)
    blk = lax.broadcasted_iota(jnp.int32, (n_sel, 1), 0)
    cur = t_row // SEL_BLOCK
    causal_blk = blk * SEL_BLOCK <= t_row
    forced = (blk == 0) | ((blk <= cur) & (blk > cur - N_LOCAL_BLOCKS))
    imp = jnp.where(forced, -MASK_VALUE, jnp.where(causal_blk, imp, MASK_VALUE))
    cnt = jnp.zeros((n_sel, tq), f32)
    for i in range(n_sel):
        ri = imp[i:i + 1, :]
        beats = jnp.where(blk > i, (ri >= imp).astype(f32), (ri > imp).astype(f32))
        cnt = cnt + beats
    topk = min(SEL_TOPK, n_sel)
    selb_ref[...] = jnp.where(cnt < topk, 0.0, MASK_VALUE)

    d0 = (lax.broadcasted_iota(jnp.int32, (tk, tq), 1)
          - lax.broadcasted_iota(jnp.int32, (tk, tq), 0))
    d0f = d0.astype(f32)
    for hh in heads:
        alibi_ref[hh] = slopes[hh] * d0f
    blocks_per_chunk = tk // SEL_BLOCK

    def reset():
        acc_ref[...] = jnp.zeros_like(acc_ref)
        for hh in heads:
            ml_ref[2 * hh, 0:1, :] = jnp.full((1, tq), MASK_VALUE, f32)
            ml_ref[2 * hh + 1, 0:1, :] = jnp.zeros((1, tq), f32)

    def chunk(k_ref, vt_ref, k0, bias, delta_f):
        kblk = k_ref[0, pl.ds(k0, tk), :]
        vt = vt_ref[0, :, pl.ds(k0, tk)]
        scores = [lax.dot_general(kblk, q_ref[0, :, hh * hd:(hh + 1) * hd], _NT, preferred_element_type=f32)
                  for hh in heads]
        for hh in heads:
            s = scores[hh] - alibi_ref[hh]
            if bias is not None:
                s = s + bias
            cst = -slopes[hh] * delta_f
            m = ml_ref[2 * hh, 0:1, :]
            l = ml_ref[2 * hh + 1, 0:1, :]
            m_new = jnp.maximum(m, jnp.max(s, axis=0, keepdims=True) + cst)
            alpha = jnp.exp(m - m_new)
            p = jnp.exp(s - (m_new - cst))
            ml_ref[2 * hh, 0:1, :] = m_new
            ml_ref[2 * hh + 1, 0:1, :] = alpha * l + jnp.sum(p, axis=0, keepdims=True)
            acc_ref[hh] = alpha * acc_ref[hh] + jnp.dot(vt, p.astype(mxu_dtype), preferred_element_type=f32)

    def finish(branch):
        for hh in heads:
            out_ref_t[hh] += gates[hh][branch:branch + 1] * (acc_ref[hh] / ml_ref[2 * hh + 1, 0:1, :])

    def sel_rows(c):
        rows = [jnp.broadcast_to(selb_ref[pl.ds(c * blocks_per_chunk + j, 1), :], (SEL_BLOCK, tq))
                for j in range(blocks_per_chunk)]
        return jnp.concatenate(rows, axis=0)

    reset()

    def sel_body(c, carry):
        k0 = pl.multiple_of(c * tk, tk)
        chunk(ksel_ref, vselt_ref, k0, sel_rows(c), (q0 - k0).astype(f32))
        return carry

    lax.fori_loop(0, qi, sel_body, 0)
    k_diag = pl.multiple_of(q0, tk)
    chunk(ksel_ref, vselt_ref, k_diag, sel_rows(qi) + jnp.where(d0 >= 0, 0.0, MASK_VALUE), 0.0)
    finish(1)

    reset()
    for delta in range(0, WINDOW + tk, tk):
        lo_ok = delta - (tk - 1) >= 0
        hi_ok = delta + (tq - 1) < WINDOW
        if lo_ok and hi_ok:
            bias = None
        else:
            dist = d0 + delta
            bias = jnp.where(dist >= 0, jnp.where(dist < WINDOW, 0.0, MASK_VALUE), MASK_VALUE)
        if delta == 0:
            chunk(kwin_ref, vwint_ref, k_diag, bias, 0.0)
        else:
            @pl.when(q0 >= delta)
            def _(delta=delta, bias=bias):
                chunk(kwin_ref, vwint_ref, pl.multiple_of(q0 - delta, tk), bias, float(delta))
    finish(2)

    for hh in heads:
        o_ref[0, :, hh * hd:(hh + 1) * hd] = out_ref_t[hh].T.astype(o_ref.dtype)


def _nsa_stage(big, small, cmp, b, s, tq=256):
    g, hd = NSA_KV_GROUPS, HEAD_DIM
    tq = min(tq, s)
    tk = tq
    n_seg = s // CMP_STRIDE
    n_sel = s // SEL_BLOCK
    big3 = big.reshape(b, s, BIG_WIDTH)
    kv_col = lambda which: (COL_KV + which * NSA_KV_WIDTH) // hd
    vt = lambda which: big3[:, :, COL_KV + which * NSA_KV_WIDTH:COL_KV + (which + 1) * NSA_KV_WIDTH] \
        .reshape(b, s, g, hd).transpose(0, 2, 3, 1).reshape(b * g, hd, s)
    vsel_t, vwin_t = vt(3), vt(5)
    kc = cmp[0]
    vc_t = cmp[1].transpose(0, 2, 1)
    gate_t = small[:, SMALL_GATE:SMALL_GATE + 3 * NSA_HEADS].reshape(b, s, g, 3 * HEADS_PER_GROUP) \
        .transpose(0, 2, 3, 1)
    c_start = np.arange(n_seg)[None, :] * CMP_STRIDE
    s_start = np.arange(n_sel)[:, None] * SEL_BLOCK
    n_cmp = (s - CMP_BLOCK) // CMP_STRIDE + 1
    ovl = ((c_start < s_start + SEL_BLOCK) & (c_start + CMP_BLOCK > s_start)
           & (np.arange(n_seg)[None, :] < n_cmp)).astype(np.float32)
    ovl = jnp.asarray(ovl, jnp.bfloat16)
    slopes = jnp.asarray((2.0 ** (-8.0 * np.arange(1, NSA_HEADS + 1) / NSA_HEADS))
                         .reshape(g, HEADS_PER_GROUP), jnp.float32)
    gw = 3 * HEADS_PER_GROUP
    kernel = functools.partial(_nsa_kernel, tq=tq, tk=tk)
    return pl.pallas_call(
        kernel,
        grid=(b, g, s // tq),
        in_specs=[
            pl.BlockSpec(memory_space=pltpu.SMEM),
            pl.BlockSpec((1, tq, HEADS_PER_GROUP * hd), lambda bi, gi, qi: (bi, qi, gi)),
            pl.BlockSpec((1, n_seg, hd), lambda bi, gi, qi: (bi * NSA_KV_GROUPS + gi, 0, 0)),
            pl.BlockSpec((1, hd, n_seg), lambda bi, gi, qi: (bi * NSA_KV_GROUPS + gi, 0, 0)),
            pl.BlockSpec((1, s, hd), lambda bi, gi, qi: (bi, 0, kv_col(2) + gi)),
            pl.BlockSpec((1, hd, s), lambda bi, gi, qi: (bi * NSA_KV_GROUPS + gi, 0, 0)),
            pl.BlockSpec((1, s, hd), lambda bi, gi, qi: (bi, 0, kv_col(4) + gi)),
            pl.BlockSpec((1, hd, s), lambda bi, gi, qi: (bi * NSA_KV_GROUPS + gi, 0, 0)),
            pl.BlockSpec((1, 1, gw, tq), lambda bi, gi, qi: (bi, gi, 0, qi)),
            pl.BlockSpec((n_sel, n_seg), lambda bi, gi, qi: (0, 0)),
        ],
        out_specs=pl.BlockSpec((1, tq, HEADS_PER_GROUP * hd), lambda bi, gi, qi: (bi, qi, gi)),
        out_shape=jax.ShapeDtypeStruct((b, s, NSA_Q_WIDTH), big.dtype),
        scratch_shapes=[
            pltpu.VMEM((n_sel, tq), jnp.float32),
            pltpu.VMEM((HEADS_PER_GROUP, tk, tq), jnp.float32),
            pltpu.VMEM((HEADS_PER_GROUP, hd, tq), jnp.float32),
            pltpu.VMEM((2 * HEADS_PER_GROUP, 8, tq), jnp.float32),
            pltpu.VMEM((HEADS_PER_GROUP, hd, tq), jnp.float32),
        ],
        compiler_params=_params(("parallel", "parallel", "arbitrary")),
    )(slopes, big3, kc, vc_t, big3, vsel_t, big3, vwin_t, gate_t, ovl)


_TN = (((0,), (0,)), ((), ()))
_HI = lax.Precision.HIGHEST


def _gla_kernel(q_ref, k_ref, v_ref, og_ref, sm_ref, wa_ref, ba_ref, gn_ref, selw_ref, o_ref,
                state_ref, *, n_chunks):
    f32 = jnp.float32
    mxu_dtype = v_ref.dtype
    c_len, sub = GLA_CHUNK, GLA_SUB
    ns = c_len // sub
    dk = GLA_KEY_DIM

    @pl.when(pl.program_id(2) == 0)
    def _():
        state_ref[...] = jnp.zeros_like(state_ref)

    tc = n_chunks * c_len
    chunks = range(n_chunks)
    row = lax.broadcasted_iota(jnp.int32, (c_len, 1), 0)
    col = lax.broadcasted_iota(jnp.int32, (1, c_len), 1)
    sub_row = row // sub
    sub_col = col // sub
    t_loc = lax.broadcasted_iota(jnp.int32, (1, sub, 1), 1)
    neg_inf = -jnp.inf

    logits = jnp.dot(sm_ref[0], wa_ref[0], precision=_HI, preferred_element_type=f32) + ba_ref[0]
    glog = (jnp.minimum(logits, 0.0) - jnp.log(1.0 + jnp.exp(-jnp.abs(logits)))) * (1.0 / GATE_TAU)
    row_b = lax.broadcasted_iota(jnp.int32, (tc, 1), 0)
    col_b = lax.broadcasted_iota(jnp.int32, (1, tc), 1)
    block_tril = jnp.where(col_b <= row_b, (row_b // c_len == col_b // c_len).astype(f32), 0.0)
    block_tril = block_tril.astype(jnp.bfloat16)
    bcum_all = jnp.zeros((tc, dk), f32)
    for piece in _split3(glog):
        bcum_all = bcum_all + jnp.dot(block_tril, piece, preferred_element_type=f32)

    rows = [slice(c * c_len, (c + 1) * c_len) for c in chunks]
    qs = [q_ref[0, rows[c], :].astype(f32) for c in chunks]
    ks = [k_ref[0, rows[c], :].astype(f32) for c in chunks]
    vs = [v_ref[0, rows[c], :] for c in chunks]
    bcums = [bcum_all[rows[c], :] for c in chunks]
    b_lasts = [b[c_len - 1:c_len, :] for b in bcums]

    q_ins, kfs, qfs, ecats, kds = [], [], [], [], []
    for c in chunks:
        q, k, bcum = qs[c], ks[c], bcums[c]
        q_ins.append((q * jnp.exp(bcum)).astype(mxu_dtype))
        r = [bcum[(j + 1) * sub - 1:(j + 1) * sub, :] for j in range(ns)]
        r_rows = jnp.concatenate([jnp.broadcast_to(rj, (sub, dk)) for rj in r], axis=0)
        kfs.append((k * jnp.exp(r_rows - bcum)).astype(mxu_dtype))
        qfs.append([(q * jnp.exp(jnp.where(row >= (j + 1) * sub, bcum - r[j], neg_inf))).astype(mxu_dtype)
                    for j in range(ns - 1)])
        q3 = q.reshape(ns, sub, dk)
        k3 = k.reshape(ns, sub, dk)
        b3 = bcum.reshape(ns, sub, dk)
        pieces = []
        for s_ in range(sub):
            arg = jnp.where(t_loc >= s_, b3 - b3[:, s_:s_ + 1, :], neg_inf)
            e = q3 * k3[:, s_:s_ + 1, :] * jnp.exp(arg)
            pieces.append(e.reshape(c_len, dk).astype(mxu_dtype))
        ecats.append(jnp.concatenate(pieces, axis=1))
        kds.append((k * jnp.exp(b_lasts[c] - bcum)).astype(mxu_dtype))

    ajs = [[lax.dot_general(qfs[c][j], kfs[c], _NT, preferred_element_type=f32) for j in range(ns - 1)]
           for c in chunks]
    d_wides = [jnp.dot(ecats[c], selw_ref[...], preferred_element_type=f32) for c in chunks]
    updates = [lax.dot_general(vs[c], kds[c], _TN, preferred_element_type=f32) for c in chunks]
    o_intras = []
    for c in chunks:
        a = jnp.where(sub_row == sub_col, d_wides[c], 0.0)
        for j in range(ns - 1):
            a = a + jnp.where(sub_col == j, ajs[c][j], 0.0)
        o_intras.append(jnp.dot(a.astype(mxu_dtype), vs[c], preferred_element_type=f32))

    st = state_ref[...]
    for c in chunks:
        o = o_intras[c] + lax.dot_general(q_ins[c], st.astype(mxu_dtype), _NT, preferred_element_type=f32)
        st = st * jnp.exp(b_lasts[c]) + updates[c]
        rms = lax.rsqrt(jnp.mean(o * o, axis=-1, keepdims=True) + EPS)
        gate = og_ref[0, rows[c], :].astype(f32)
        y = o * rms * gn_ref[...] * (gate * jax.nn.sigmoid(gate))
        o_ref[0, rows[c], :] = y.astype(o_ref.dtype)
    state_ref[...] = st


def _gla_stage(big, small, w_alpha2, b_alpha, g_norm, b, s, tc=512):
    h, dk, dv = GLA_HEADS, GLA_KEY_DIM, GLA_VAL_DIM
    tc = min(tc, s)
    big3 = big.reshape(b, s, BIG_WIDTH)
    small3 = small.reshape(b, s, SMALL_WIDTH)
    wa = jnp.zeros((h, SMALL_WIDTH, dk), jnp.float32).at[:, SMALL_ALPHA:SMALL_ALPHA + GATE_RANK, :].set(
        w_alpha2.astype(jnp.float32).reshape(GATE_RANK, h, dk).transpose(1, 0, 2))
    ba = b_alpha.astype(jnp.float32).reshape(h, 1, dk)
    gn = g_norm.astype(jnp.float32).reshape(1, dv)
    sel = (np.arange(GLA_SUB * dk)[:, None] // dk == np.arange(GLA_CHUNK)[None, :] % GLA_SUB)
    selw = jnp.asarray(sel.astype(np.float32), big.dtype)
    kernel = functools.partial(_gla_kernel, n_chunks=tc // GLA_CHUNK)
    return pl.pallas_call(
        kernel,
        grid=(b, h, s // tc),
        in_specs=[
            pl.BlockSpec((1, tc, dk), lambda bi, hi, ci: (bi, ci, COL_GLA_Q // dk + hi)),
            pl.BlockSpec((1, tc, dk), lambda bi, hi, ci: (bi, ci, COL_GLA_K // dk + hi)),
            pl.BlockSpec((1, tc, dv), lambda bi, hi, ci: (bi, ci, COL_GLA_V // dv + hi)),
            pl.BlockSpec((1, tc, dv), lambda bi, hi, ci: (bi, ci, COL_GLA_GATE // dv + hi)),
            pl.BlockSpec((1, tc, SMALL_WIDTH), lambda bi, hi, ci: (bi, ci, 0)),
            pl.BlockSpec((1, SMALL_WIDTH, dk), lambda bi, hi, ci: (hi, 0, 0)),
            pl.BlockSpec((1, 1, dk), lambda bi, hi, ci: (hi, 0, 0)),
            pl.BlockSpec((1, dv), lambda bi, hi, ci: (0, 0)),
            pl.BlockSpec((GLA_SUB * dk, GLA_CHUNK), lambda bi, hi, ci: (0, 0)),
        ],
        out_specs=pl.BlockSpec((1, tc, dv), lambda bi, hi, ci: (bi, ci, hi)),
        out_shape=jax.ShapeDtypeStruct((b, s, GLA_V_WIDTH), big.dtype),
        scratch_shapes=[pltpu.VMEM((dv, dk), jnp.float32)],
        compiler_params=_params(("parallel", "parallel", "arbitrary")),
    )(big3, big3, big3, big3, small3, wa, ba, gn, selw)


ROUTE_E0 = N_EXPERT_GROUPS
ROUTE_ID = ROUTE_E0 + N_EXPERTS
ROUTE_W = ROUTE_ID + 2
MOE_TM = 256
GATHER_UNROLL = 8


def _out_proj_kernel(x_ref, nsa_ref, gla_ref, wt_ref, wb_ref, g_ref, wr_ref, br_ref,
                     x1_ref, h_ref, comb_ref):
    f32 = jnp.float32
    acc = jnp.dot(nsa_ref[...], wt_ref[...], preferred_element_type=f32)
    acc = acc + jnp.dot(gla_ref[...], wb_ref[...], preferred_element_type=f32)
    x1 = x_ref[...] + acc
    x1_ref[...] = x1
    hf = x1 * lax.rsqrt(jnp.mean(x1 * x1, axis=-1, keepdims=True) + EPS) * g_ref[...]
    h_ref[...] = hf.astype(h_ref.dtype)

    h_hi, h_mid, _ = _split3(hf)
    logits = (jnp.dot(h_hi, wr_ref[0], preferred_element_type=f32)
              + jnp.dot(h_hi, wr_ref[1], preferred_element_type=f32)
              + jnp.dot(h_mid, wr_ref[0], preferred_element_type=f32)) + br_ref[...]

    lane = lax.broadcasted_iota(jnp.int32, (1, LANES), 1).astype(f32)
    big_lane = float(LANES)
    neg_inf = -jnp.inf
    first_argmax = lambda vals, vmax: jnp.min(jnp.where(vals == vmax, lane, big_lane), axis=-1, keepdims=True)
    gl = jnp.where(lane < N_EXPERT_GROUPS, logits, neg_inf)
    gmax = jnp.max(gl, axis=-1, keepdims=True)
    gsel = first_argmax(gl, gmax)
    gw = 1.0 / jnp.sum(jnp.exp(gl - gmax), axis=-1, keepdims=True)
    lo = ROUTE_E0 + EXPERTS_PER_GROUP * gsel
    el = jnp.where(lane >= lo, jnp.where(lane < lo + EXPERTS_PER_GROUP, logits, neg_inf), neg_inf)
    m1 = jnp.max(el, axis=-1, keepdims=True)
    i1 = first_argmax(el, m1)
    el2 = jnp.where(lane == i1, neg_inf, el)
    m2 = jnp.max(el2, axis=-1, keepdims=True)
    i2 = first_argmax(el2, m2)
    e2 = jnp.exp(m2 - m1)
    w1 = gw / (1.0 + e2)
    w2 = w1 * e2
    onehot = jnp.where(lane == i1, 1.0, 0.0) + jnp.where(lane == i2, 1.0, 0.0)
    meta = (jnp.where(lane == ROUTE_ID, i1 - ROUTE_E0, 0.0) + jnp.where(lane == ROUTE_ID + 1, i2 - ROUTE_E0, 0.0)
            + jnp.where(lane == ROUTE_W, w1, 0.0) + jnp.where(lane == ROUTE_W + 1, w2, 0.0))
    comb_ref[...] = onehot + meta


def _out_proj_stage(x2, nsa, gla, w_out, g_ffn, w_rg, b_rg, w_re, b_re, tm=512):
    t, d = x2.shape
    tm = min(tm, t)
    half = nsa.shape[1]
    w_top = w_out[:half].astype(nsa.dtype)
    w_bot = w_out[half:].astype(nsa.dtype)
    wr = jnp.concatenate([w_rg, w_re.reshape(d, N_EXPERTS),
                          jnp.zeros((d, LANES - ROUTE_E0 - N_EXPERTS), jnp.float32)], axis=1)
    wr_hi, wr_mid, _ = _split3(wr)
    wr2 = jnp.stack([wr_hi, wr_mid])
    br = jnp.concatenate([b_rg, b_re.reshape(N_EXPERTS),
                          jnp.zeros((LANES - ROUTE_E0 - N_EXPERTS,), jnp.float32)])[None]
    row = lambda m: (m, 0)
    fixed = lambda m: (0, 0)
    return pl.pallas_call(
        _out_proj_kernel,
        grid=(t // tm,),
        in_specs=[
            pl.BlockSpec((tm, d), row),
            pl.BlockSpec((tm, half), row),
            pl.BlockSpec((tm, half), row),
            pl.BlockSpec((half, d), fixed),
            pl.BlockSpec((half, d), fixed),
            pl.BlockSpec((1, d), fixed),
            pl.BlockSpec((2, d, LANES), lambda m: (0, 0, 0)),
            pl.BlockSpec((1, LANES), fixed),
        ],
        out_specs=[pl.BlockSpec((tm, d), row), pl.BlockSpec((tm, d), row), pl.BlockSpec((tm, LANES), row)],
        out_shape=[
            jax.ShapeDtypeStruct((t, d), jnp.float32),
            jax.ShapeDtypeStruct((t, d), jnp.float32),
            jax.ShapeDtypeStruct((t, LANES), jnp.float32),
        ],
        compiler_params=_params(("parallel",)),
    )(x2, nsa, gla, w_top, w_bot, g_ffn.astype(jnp.float32)[None], wr2, br)


def _rank_kernel(route_ref, base_ref, rank_ref, count_ref, carry_ref):
    f32 = jnp.float32
    tm = route_ref.shape[0]

    @pl.when(pl.program_id(0) == 0)
    def _():
        carry_ref[...] = base_ref[...]

    route = route_ref[...]
    lane = lax.broadcasted_iota(jnp.int32, (1, LANES), 1)
    lane_f = lane.astype(f32)
    onehot = jnp.where((lane >= ROUTE_E0) & (lane < ROUTE_ID), route, 0.0)
    row = lax.broadcasted_iota(jnp.int32, (tm, tm), 0)
    col = lax.broadcasted_iota(jnp.int32, (tm, tm), 1)
    strict_lower = (col < row).astype(jnp.bfloat16)
    before = jnp.dot(strict_lower, onehot.astype(jnp.bfloat16), preferred_element_type=f32) + carry_ref[0:1, :]
    ranks = []
    for j in range(2):
        lane_j = route[:, ROUTE_ID + j:ROUTE_ID + j + 1] + float(ROUTE_E0)
        ranks.append(jnp.sum(jnp.where(lane_f == lane_j, before, 0.0), axis=-1, keepdims=True))
    rank_ref[...] = jnp.where(lane == 0, ranks[0], jnp.where(lane == 1, ranks[1], 0.0))
    total = carry_ref[0:1, :] + jnp.sum(onehot, axis=0, keepdims=True)
    carry_ref[...] = jnp.broadcast_to(total, carry_ref.shape)
    count_ref[...] = jnp.broadcast_to(total, count_ref.shape)


def _rank_stage(route, base, tm=512):
    t = route.shape[0]
    tm = min(tm, t)
    return pl.pallas_call(
        _rank_kernel,
        grid=(t // tm,),
        in_specs=[pl.BlockSpec((tm, LANES), lambda m: (m, 0)), pl.BlockSpec((8, LANES), lambda m: (0, 0))],
        out_specs=[pl.BlockSpec((tm, LANES), lambda m: (m, 0)), pl.BlockSpec((8, LANES), lambda m: (0, 0))],
        out_shape=[jax.ShapeDtypeStruct((t, LANES), jnp.float32), jax.ShapeDtypeStruct((8, LANES), jnp.float32)],
        scratch_shapes=[pltpu.VMEM((8, LANES), jnp.float32)],
        compiler_params=_params(("arbitrary",)),
    )(route, base)


def _gather_rows(idx_ref, n_rows, src_hbm, dst, sem):
    def body(r, carry):
        tok = idx_ref[0, 0, r]
        pltpu.make_async_copy(src_hbm.at[pl.ds(tok, 1), :], dst.at[pl.ds(r, 1), :], sem).start()
        return carry
    lax.fori_loop(0, n_rows, body, 0, unroll=GATHER_UNROLL)


def _wait_rows(n_rows, src_hbm, dst, sem):
    pltpu.make_async_copy(src_hbm.at[pl.ds(0, n_rows), :], dst, sem).wait()


def _moe_kernel(te_ref, na_ref, idx_ref, idx_next_ref, h_hbm, wg_ref, wu_ref, wd_ref, y_ref,
                buf, sem, wg_bf, wu_bf, wd_bf):
    f32 = jnp.float32
    bf16 = jnp.bfloat16
    i = pl.program_id(0)
    n_active = na_ref[0]
    tm = buf.shape[1]
    slot = i % 2

    @pl.when(i == 0)
    def _():
        _gather_rows(idx_ref, tm, h_hbm, buf.at[0], sem.at[0])

    @pl.when(i < n_active)
    def _():
        expert_changed = (i == 0) | (te_ref[i] != te_ref[jnp.maximum(i - 1, 0)])

        @pl.when(expert_changed)
        def _():
            wg_bf[...] = wg_ref[0].astype(bf16)
            wu_bf[...] = wu_ref[0].astype(bf16)
            wd_bf[...] = wd_ref[0].astype(bf16)

        for cur in range(2):
            @pl.when(slot == cur)
            def _(cur=cur):
                nxt = 1 - cur
                for r in range(tm):
                    tok = idx_next_ref[0, 0, r]
                    pltpu.make_async_copy(h_hbm.at[pl.ds(tok, 1), :], buf.at[nxt, pl.ds(r, 1), :],
                                          sem.at[nxt]).start()
                _wait_rows(tm, h_hbm, buf.at[cur], sem.at[cur])
                xb = buf[cur].astype(bf16)
                gate = jnp.dot(xb, wg_bf[...], preferred_element_type=f32)
                up = jnp.dot(xb, wu_bf[...], preferred_element_type=f32)
                hid = (gate * jax.nn.sigmoid(gate) * up).astype(bf16)
                y_ref[...] = jnp.dot(hid, wd_bf[...], preferred_element_type=f32)

    @pl.when(i == n_active)
    def _():
        _wait_rows(tm, h_hbm, buf.at[slot], sem.at[slot])

    @pl.when(i >= n_active)
    def _():
        y_ref[...] = jnp.zeros_like(y_ref)


def _moe_stage(h2, tok_of_pos, tile_expert, n_active, w_gate, w_up, w_down):
    t, d = h2.shape
    n_tiles, _, tm = tok_of_pos.shape
    f = w_gate.shape[-1]
    wg = w_gate.reshape(N_EXPERTS, d, f)
    wu = w_up.reshape(N_EXPERTS, d, f)
    wd = w_down.reshape(N_EXPERTS, f, d)
    grid_spec = pltpu.PrefetchScalarGridSpec(
        num_scalar_prefetch=2,
        grid=(n_tiles,),
        in_specs=[
            pl.BlockSpec((1, 1, tm), lambda i, te, na: (i, 0, 0), memory_space=pltpu.SMEM),
            pl.BlockSpec((1, 1, tm), lambda i, te, na: (jnp.minimum(i + 1, n_tiles - 1), 0, 0),
                         memory_space=pltpu.SMEM),
            pl.BlockSpec(memory_space=pl.ANY),
            pl.BlockSpec((1, d, f), lambda i, te, na: (te[i], 0, 0)),
            pl.BlockSpec((1, d, f), lambda i, te, na: (te[i], 0, 0)),
            pl.BlockSpec((1, f, d), lambda i, te, na: (te[i], 0, 0)),
        ],
        out_specs=pl.BlockSpec((tm, d), lambda i, te, na: (i, 0)),
        scratch_shapes=[
            pltpu.VMEM((2, tm, d), jnp.float32), pltpu.SemaphoreType.DMA((2,)),
            pltpu.VMEM((d, f), jnp.bfloat16), pltpu.VMEM((d, f), jnp.bfloat16), pltpu.VMEM((f, d), jnp.bfloat16),
        ],
    )
    return pl.pallas_call(
        _moe_kernel,
        grid_spec=grid_spec,
        out_shape=jax.ShapeDtypeStruct((n_tiles * tm, d), jnp.float32),
        compiler_params=_params(("arbitrary",)),
    )(tile_expert, n_active, tok_of_pos, tok_of_pos, h2, wg, wu, wd)


def _combine_kernel(p1_ref, p2_ref, p1n_ref, p2n_ref, route_ref, x1_ref, gf_ref, y_hbm, o_ref, buf, sem,
                    *, final_norm):
    i = pl.program_id(0)
    n = pl.num_programs(0)
    tm = buf.shape[2]

    def gather(a_ref, b_ref, slot):
        for j, idx_ref in enumerate((a_ref, b_ref)):
            for r in range(tm):
                pltpu.make_async_copy(y_hbm.at[pl.ds(idx_ref[0, 0, r], 1), :],
                                      buf.at[slot, j, pl.ds(r, 1), :], sem.at[slot, j]).start()

    @pl.when(i == 0)
    def _():
        gather(p1_ref, p2_ref, 0)

    @pl.when(i + 1 < n)
    def _():
        gather(p1n_ref, p2n_ref, (i + 1) % 2)

    slot = i % 2
    _wait_rows(tm, y_hbm, buf.at[slot, 0], sem.at[slot, 0])
    _wait_rows(tm, y_hbm, buf.at[slot, 1], sem.at[slot, 1])
    route = route_ref[...]
    w1 = route[:, ROUTE_W:ROUTE_W + 1]
    w2 = route[:, ROUTE_W + 1:ROUTE_W + 2]
    y = x1_ref[...] + (w1 * buf[slot, 0] + w2 * buf[slot, 1])
    if final_norm:
        y = y * lax.rsqrt(jnp.mean(y * y, axis=-1, keepdims=True) + EPS) * gf_ref[...]
    o_ref[...] = y


def _combine_stage(y_sorted, pos1, pos2, route, x1, g_final, final_norm, tm=256):
    t, d = x1.shape
    tm = min(tm, t)
    n = t // tm
    p1 = pos1.reshape(n, 1, tm)
    p2 = pos2.reshape(n, 1, tm)
    cur = lambda i: (i, 0, 0)
    nxt = lambda i: (jnp.minimum(i + 1, n - 1), 0, 0)
    smem = lambda index_map: pl.BlockSpec((1, 1, tm), index_map, memory_space=pltpu.SMEM)
    return pl.pallas_call(
        functools.partial(_combine_kernel, final_norm=final_norm),
        grid=(n,),
        in_specs=[
            smem(cur), smem(cur), smem(nxt), smem(nxt),
            pl.BlockSpec((tm, LANES), lambda i: (i, 0)),
            pl.BlockSpec((tm, d), lambda i: (i, 0)),
            pl.BlockSpec((1, d), lambda i: (0, 0)),
            pl.BlockSpec(memory_space=pl.ANY),
        ],
        out_specs=pl.BlockSpec((tm, d), lambda i: (i, 0)),
        out_shape=jax.ShapeDtypeStruct((t, d), jnp.float32),
        scratch_shapes=[pltpu.VMEM((2, 2, tm, d), jnp.float32), pltpu.SemaphoreType.DMA((2, 2))],
        compiler_params=_params(("arbitrary",)),
    )(p1, p2, p1, p2, route, x1, g_final.astype(jnp.float32)[None], y_sorted)


def _routing_tables(route, tm):
    t = route.shape[0]
    zero_base = jnp.zeros((8, LANES), jnp.float32)
    _, count = _rank_stage(route, zero_base)
    counts = count[0, ROUTE_E0:ROUTE_E0 + N_EXPERTS].astype(jnp.int32)
    padded = ((counts + tm - 1) // tm) * tm
    ends = jnp.cumsum(padded)
    offs = ends - padded
    base = zero_base.at[:, ROUTE_E0:ROUTE_E0 + N_EXPERTS].set(
        jnp.broadcast_to(offs.astype(jnp.float32), (8, N_EXPERTS)))
    pos, _ = _rank_stage(route, base)
    pos1 = pos[:, 0].astype(jnp.int32)
    pos2 = pos[:, 1].astype(jnp.int32)
    n_tiles = (2 * t) // tm + N_EXPERTS
    tok = jnp.arange(t, dtype=jnp.int32)
    tok_of_pos = jnp.zeros((n_tiles * tm,), jnp.int32).at[jnp.concatenate([pos1, pos2])].set(
        jnp.concatenate([tok, tok]), unique_indices=True)
    tile_start = jnp.arange(n_tiles, dtype=jnp.int32) * tm
    tile_expert = jnp.minimum(jnp.sum((ends[None, :] <= tile_start[:, None]).astype(jnp.int32), axis=1),
                              N_EXPERTS - 1)
    n_active = (ends[-1] // tm).astype(jnp.int32).reshape(1)
    return pos1, pos2, tok_of_pos.reshape(n_tiles, 1, tm), tile_expert, n_active


def kernel(x, g_mix_norm, w_in, b_nsa_gate, cmp_pos_k, w_cmp_k1, w_cmp_k2, cmp_pos_v, w_cmp_v1, w_cmp_v2,
           w_alpha2, b_alpha, g_gla_norm, w_out, g_ffn_norm, w_router_group, b_router_group,
           w_router_expert, b_router_expert, w_expert_gate, w_expert_up, w_expert_down, g_final_norm):
    b, s, d = x.shape
    depth = w_in.shape[0]
    x2 = x.reshape(b * s, d)
    for l in range(depth):
        last = l == depth - 1
        w_big, w_small, col_scale, small_bias = _prep_in_proj_weights(w_in[l], b_nsa_gate[l])
        big, small = _in_proj(x2, g_mix_norm[l].astype(jnp.float32)[None], w_big, w_small, col_scale, small_bias)
        cmp = _compress_stage(big, b, s, cmp_pos_k[l], w_cmp_k1[l], w_cmp_k2[l],
                              cmp_pos_v[l], w_cmp_v1[l], w_cmp_v2[l])
        nsa = _nsa_stage(big, small, cmp, b, s).reshape(b * s, NSA_Q_WIDTH)
        gla = _gla_stage(big, small, w_alpha2[l], b_alpha[l], g_gla_norm[l], b, s).reshape(b * s, GLA_V_WIDTH)
        x1, h2, route = _out_proj_stage(x2, nsa, gla, w_out[l], g_ffn_norm[l], w_router_group[l],
                                        b_router_group[l], w_router_expert[l], b_router_expert[l])
        pos1, pos2, tok_of_pos, tile_expert, n_active = _routing_tables(route, MOE_TM)
        y_sorted = _moe_stage(h2, tok_of_pos, tile_expert, n_active,
                              w_expert_gate[l], w_expert_up[l], w_expert_down[l])
        x2 = _combine_stage(y_sorted, pos1, pos2, route, x1, g_final_norm, last)
    return x2.reshape(b, s, d)
```
